```python
import jax, jax.numpy as jnp
from jax import lax
import numpy as np

D_MODEL = 1024
BATCH = 8
SEQ = 8192
DEPTH = 4
DEC_BATCH = 16
DEC_SEQ = 32
PAST_LEN = 1024

CHUNK = 64
N_PAST_CHUNKS = 8
A_PAST = N_PAST_CHUNKS * CHUNK
BAND = A_PAST + CHUNK
MAX_REL = 256
REL_SIZE = CHUNK + MAX_REL
N_HEADS = 16
HEAD_DIM = D_MODEL // N_HEADS
D_FF = -(-8 * D_MODEL // (3 * 256)) * 256
N_A_LAYERS = (DEPTH + 1) // 2
N_B_LAYERS = DEPTH // 2
Q_BLOCK = 128
RMS_EPS = 1e-6
NEG_INF = -1e30
ATTN_SCALE = HEAD_DIM ** -0.5

kernel_name = 'streaming_hybrid_chunkrel_fox_step'


def rms_norm(x, g):
    xf = x.astype(jnp.float32)
    y = xf * lax.rsqrt(jnp.mean(xf * xf, axis=-1, keepdims=True) + RMS_EPS)
    return (y * g.astype(jnp.float32)).astype(x.dtype)


def attend(q, k, v, bias, mask):
    s = jnp.einsum('bqhd,bkhd->bhqk', q, k).astype(jnp.float32) * ATTN_SCALE + bias
    if mask is not None:
        s = jnp.where(mask, s, NEG_INF)
    p = jax.nn.softmax(s, axis=-1).astype(v.dtype)
    return jnp.einsum('bhqk,bkhd->bqhd', p, v)


def rel_bias_lookup(table, rel):
    idx = jnp.clip(rel, -(CHUNK - 1), MAX_REL) + (CHUNK - 1)
    return table[:, idx].astype(jnp.float32)


def chunk_attn_prompt(q, k, v, table):
    B, S, H, Dh = q.shape
    pad = ((0, 0), (A_PAST, 0), (0, 0), (0, 0))
    kp = jnp.pad(k, pad)
    vp = jnp.pad(v, pad)
    band = jnp.arange(BAND)
    rel = (A_PAST + jnp.arange(CHUNK))[:, None] - band[None, :]
    bias = rel_bias_lookup(table, rel)

    def one_chunk(ci):
        start = ci * CHUNK
        qc = lax.dynamic_slice_in_dim(q, start, CHUNK, axis=1)
        kc = lax.dynamic_slice_in_dim(kp, start, BAND, axis=1)
        vc = lax.dynamic_slice_in_dim(vp, start, BAND, axis=1)
        valid = (start + band) >= A_PAST
        return attend(qc, kc, vc, bias, valid)

    out = lax.map(one_chunk, jnp.arange(S // CHUNK))
    return jnp.moveaxis(out, 0, 1).reshape(B, S, H, Dh)


def chunk_attn_sample(q, k, v, k_cache, v_cache, table):
    L = k_cache.shape[1]
    T = q.shape[1]
    kk = jnp.concatenate([k_cache, k], axis=1)
    vv = jnp.concatenate([v_cache, v], axis=1)
    kpos = jnp.concatenate([jnp.arange(L) - L, jnp.arange(T)])
    rel = jnp.arange(T)[:, None] - kpos[None, :]
    return attend(q, kk, vv, rel_bias_lookup(table, rel), None)


def forget_attn_prompt(q, k, v, logf):
    B, S, H, Dh = q.shape
    F = jnp.cumsum(logf, axis=1).transpose(0, 2, 1)
    kpos = jnp.arange(S)

    def one_block(bi):
        start = bi * Q_BLOCK
        qb = lax.dynamic_slice_in_dim(q, start, Q_BLOCK, axis=1)
        Fq = lax.dynamic_slice_in_dim(F, start, Q_BLOCK, axis=2)
        bias = Fq[:, :, :, None] - F[:, :, None, :]
        mask = kpos[None, :] <= (start + jnp.arange(Q_BLOCK))[:, None]
        return attend(qb, k, v, bias, mask)

    out = lax.map(one_block, jnp.arange(S // Q_BLOCK))
    return jnp.moveaxis(out, 0, 1).reshape(B, S, H, Dh)


def forget_attn_sample(q, k, v, logf, k_cache, v_cache, logf_cache):
    L = k_cache.shape[1]
    T = q.shape[1]
    lf = jnp.concatenate([logf_cache.astype(jnp.float32), logf], axis=1)
    F = jnp.cumsum(lf, axis=1).transpose(0, 2, 1)
    bias = F[:, :, L:, None] - F[:, :, None, :]
    mask = jnp.arange(L + T)[None, :] <= (L + jnp.arange(T))[:, None]
    kk = jnp.concatenate([k_cache, k], axis=1)
    vv = jnp.concatenate([v_cache, v], axis=1)
    return attend(q, kk, vv, bias, mask)


def trunk(x, c, a_mix, b_mix, w_mod, b_mod, g_mix, g_ffn, w_qkv, w_o,
          w_fgate, b_fgate, w_gu, w_down, g_out):
    B, S, _ = x.shape
    a_state, b_state = [], []
    cs = jax.nn.silu(c)
    for l in range(DEPTH):
        mod = (cs @ w_mod[l] + b_mod[l])[:, None, :]
        sh_m, sc_m, ga_m, sh_f, sc_f, ga_f = jnp.split(mod, 6, axis=-1)
        h = rms_norm(x, g_mix[l]) * (1 + sc_m) + sh_m
        qkv = (h @ w_qkv[l]).reshape(B, S, 3, N_HEADS, HEAD_DIM)
        q, k, v = qkv[:, :, 0], qkv[:, :, 1], qkv[:, :, 2]
        if l % 2 == 0:
            o, st = a_mix(l // 2, q, k, v)
            a_state.append(st)
        else:
            ib = l // 2
            logf = jax.nn.log_sigmoid((h @ w_fgate[ib]).astype(jnp.float32)
                                      + b_fgate[ib].astype(jnp.float32))
            o, st = b_mix(ib, q, k, v, logf)
            b_state.append(st)
        x = x + ga_m * (o.reshape(B, S, D_MODEL) @ w_o[l])
        h = rms_norm(x, g_ffn[l]) * (1 + sc_f) + sh_f
        gate, up = jnp.split(h @ w_gu[l], 2, axis=-1)
        x = x + ga_f * ((jax.nn.silu(gate) * up) @ w_down[l])
    return rms_norm(x, g_out), a_state, b_state


def setup_inputs(seed: int = 0) -> dict:
    key = jax.random.key(seed)
    ks = jax.random.split(key, 24)
    D = D_MODEL
    a_len = min(A_PAST, PAST_LEN)
    sd = D ** -0.5
    return {
        'x_prompt': jax.random.normal(ks[0], (BATCH, SEQ, D), jnp.float32),
        'x_sample': jax.random.normal(ks[1], (DEC_BATCH, DEC_SEQ, D), jnp.float32),
        'cache_a_k': jax.random.normal(ks[2], (N_A_LAYERS, DEC_BATCH, a_len, N_HEADS, HEAD_DIM), jnp.float32),
        'cache_a_v': jax.random.normal(ks[3], (N_A_LAYERS, DEC_BATCH, a_len, N_HEADS, HEAD_DIM), jnp.float32),
        'cache_b_k': jax.random.normal(ks[4], (N_B_LAYERS, DEC_BATCH, PAST_LEN, N_HEADS, HEAD_DIM), jnp.float32),
        'cache_b_v': jax.random.normal(ks[5], (N_B_LAYERS, DEC_BATCH, PAST_LEN, N_HEADS, HEAD_DIM), jnp.float32),
        'cache_b_logf': jax.nn.log_sigmoid(
            jax.random.uniform(ks[6], (N_B_LAYERS, DEC_BATCH, PAST_LEN, N_HEADS), jnp.float32, 1.0, 4.0)
            + 0.5 * jax.random.normal(ks[7], (N_B_LAYERS, DEC_BATCH, PAST_LEN, N_HEADS), jnp.float32)),
        'c_prompt': jax.random.normal(ks[8], (BATCH, D), jnp.float32),
        'c_sample': jax.random.normal(ks[9], (DEC_BATCH, D), jnp.float32),
        'w_mod': 0.5 * sd * jax.random.normal(ks[10], (DEPTH, D, 6 * D), jnp.float32),
        'b_mod': 0.02 * jax.random.normal(ks[11], (DEPTH, 6 * D), jnp.float32),
        'g_mix': 1.0 + 0.02 * jax.random.normal(ks[12], (DEPTH, D), jnp.float32),
        'g_ffn': 1.0 + 0.02 * jax.random.normal(ks[13], (DEPTH, D), jnp.float32),
        'w_qkv': sd * jax.random.normal(ks[14], (DEPTH, D, 3 * D), jnp.float32),
        'w_o': sd * jax.random.normal(ks[15], (DEPTH, D, D), jnp.float32),
        'rel_bias': 0.1 * jax.random.normal(ks[16], (N_A_LAYERS, N_HEADS, REL_SIZE), jnp.float32),
        'w_fgate': 0.5 * sd * jax.random.normal(ks[17], (N_B_LAYERS, D, N_HEADS), jnp.float32),
        'b_fgate': jax.random.uniform(ks[18], (N_B_LAYERS, N_HEADS), jnp.float32, 1.0, 4.0),
        'w_gu': sd * jax.random.normal(ks[19], (DEPTH, D, 2 * D_FF), jnp.float32),
        'w_down': (D_FF ** -0.5) * jax.random.normal(ks[20], (DEPTH, D_FF, D), jnp.float32),
        'g_out': 1.0 + 0.02 * jax.random.normal(ks[21], (D,), jnp.float32),
    }


def reference(x_prompt, x_sample, cache_a_k, cache_a_v, cache_b_k, cache_b_v, cache_b_logf,
              c_prompt, c_sample, w_mod, b_mod, g_mix, g_ffn, w_qkv, w_o, rel_bias,
              w_fgate, b_fgate, w_gu, w_down, g_out):
    keep = min(A_PAST, x_prompt.shape[1])

    def a_prompt(i, q, k, v):
        return chunk_attn_prompt(q, k, v, rel_bias[i]), (k[:, -keep:], v[:, -keep:])

    def b_prompt(i, q, k, v, logf):
        return forget_attn_prompt(q, k, v, logf), (k, v, logf)

    def a_sample(i, q, k, v):
        return chunk_attn_sample(q, k, v, cache_a_k[i], cache_a_v[i], rel_bias[i]), (k, v)

    def b_sample(i, q, k, v, logf):
        o = forget_attn_sample(q, k, v, logf, cache_b_k[i], cache_b_v[i], cache_b_logf[i])
        return o, (k, v, logf)

    y_prompt, pa, pb = trunk(x_prompt, c_prompt, a_prompt, b_prompt, w_mod, b_mod, g_mix, g_ffn,
                             w_qkv, w_o, w_fgate, b_fgate, w_gu, w_down, g_out)
    y_sample, sa, sb = trunk(x_sample, c_sample, a_sample, b_sample, w_mod, b_mod, g_mix, g_ffn,
                             w_qkv, w_o, w_fgate, b_fgate, w_gu, w_down, g_out)

    prompt_a_k = jnp.stack([s[0] for s in pa])
    prompt_a_v = jnp.stack([s[1] for s in pa])
    prompt_b_k = jnp.stack([s[0] for s in pb])
    prompt_b_v = jnp.stack([s[1] for s in pb])
    prompt_b_logf = jnp.stack([s[2] for s in pb])
    sample_a_k = jnp.stack([s[0] for s in sa])
    sample_a_v = jnp.stack([s[1] for s in sa])
    sample_b_k = jnp.stack([s[0] for s in sb])
    sample_b_v = jnp.stack([s[1] for s in sb])
    sample_b_logf = jnp.stack([s[2] for s in sb])
    return (y_prompt, y_sample, prompt_a_k, prompt_a_v, prompt_b_k, prompt_b_v, prompt_b_logf,
            sample_a_k, sample_a_v, sample_b_k, sample_b_v, sample_b_logf)
```

```python
import functools

import jax
import jax.numpy as jnp
from jax import lax
from jax.experimental import pallas as pl
from jax.experimental.pallas import tpu as pltpu

F32 = jnp.float32
BF16 = jnp.bfloat16

N_HEADS = 16
HEAD_DIM = 64
CHUNK = 64
N_PAST_CHUNKS = 8
A_PAST = N_PAST_CHUNKS * CHUNK
MAX_REL = 256
RMS_EPS = 1e-6
NEG_INF = -1e30
ATTN_SCALE = HEAD_DIM ** -0.5

LANES = 128
HEADS_PER_BLOCK = LANES // HEAD_DIM
VMEM_LIMIT_BYTES = 48 * 1024 * 1024

ROW_TILE = 512
A_TILE = 256
A_WIN = A_TILE + A_PAST
FOX_TILE = 512
FF_CHUNK = 256
REL_PAD = 384
G_WIDTH = 1024


def _params(*sem):
    return pltpu.CompilerParams(dimension_semantics=sem, vmem_limit_bytes=VMEM_LIMIT_BYTES)


def _resident(shape):
    zeros = (0,) * len(shape)
    return pl.BlockSpec(shape, lambda *_: zeros, pipeline_mode=pl.Buffered(1))


def _nt_dot(a, b):
    return lax.dot_general(a, b, (((1,), (1,)), ((), ())), preferred_element_type=F32)


def _modulated_norm(x, g, scale, shift):
    ms = jnp.mean(x * x, axis=-1, keepdims=True)
    y = x * lax.rsqrt(ms + RMS_EPS) * g
    return y * (1.0 + scale) + shift


def _split3(x):
    hi = x.astype(BF16)
    r1 = x - hi.astype(F32)
    mid = r1.astype(BF16)
    lo = (r1 - mid.astype(F32)).astype(BF16)
    return hi, mid, lo


def _mod_body(c_ref, w_ref, b_ref, o_ref):
    cs = jax.nn.silu(c_ref[...]).astype(BF16)
    o_ref[0] = jnp.dot(cs, w_ref[0].astype(BF16), preferred_element_type=F32) + b_ref[0]


def _modulation(c, w_mod, b_mod):
    depth, d, n = w_mod.shape
    rows = c.shape[0]
    tn = 1536
    assert n % tn == 0 and rows % 16 == 0
    return pl.pallas_call(
        _mod_body,
        grid=(depth, n // tn),
        in_specs=[pl.BlockSpec((rows, d), lambda l, j: (0, 0)),
                  pl.BlockSpec((1, d, tn), lambda l, j: (l, 0, j)),
                  pl.BlockSpec((1, 1, tn), lambda l, j: (l, 0, j))],
        out_specs=pl.BlockSpec((1, rows, tn), lambda l, j: (l, 0, j)),
        out_shape=jax.ShapeDtypeStruct((depth, rows, n), F32),
        compiler_params=_params("parallel", "parallel"),
        name="modulation",
    )(c, w_mod, b_mod.reshape(depth, 1, n))


def _qkv_body(*refs, d, emit_qkv, emit_kv32, emit_logf):
    x_ref, mod_ref, g_ref, w_ref = refs[:4]
    pos = 4
    if emit_logf:
        wf_ref, bf_ref = refs[4:6]
        pos = 6
    outs = refs[pos:]
    h = _modulated_norm(x_ref[0], g_ref[...], mod_ref[0, 1:2, :], mod_ref[0, 0:1, :])
    hb = h.astype(BF16)
    oi = 0
    if emit_qkv:
        q = jnp.dot(hb, w_ref[:, 0:d], preferred_element_type=F32)
        outs[0][0] = (q * ATTN_SCALE).astype(BF16)
    k = jnp.dot(hb, w_ref[:, d:2 * d], preferred_element_type=F32)
    v = jnp.dot(hb, w_ref[:, 2 * d:3 * d], preferred_element_type=F32)
    if emit_qkv:
        outs[1][0] = k.astype(BF16)
        outs[2][0] = v.astype(BF16)
        oi = 3
    if emit_kv32:
        outs[oi][0] = k
        outs[oi + 1][0] = v
        oi += 2
    if emit_logf:
        z = jnp.dot(hb, wf_ref[...], preferred_element_type=F32) + bf_ref[...]
        outs[oi][0] = jax.nn.log_sigmoid(z)


def _qkv_proj(x, mod, g, w_qkv, wf=None, bf=None, *, ts, emit_qkv=True, emit_kv32=False,
              row_start=0, rows=None):
    b, s, d = x.shape
    rows = s if rows is None else rows
    assert rows % ts == 0 and row_start % ts == 0
    off = row_start // ts
    emit_logf = wf is not None
    row_spec = lambda width: pl.BlockSpec((1, ts, width), lambda i, j: (i, j, 0))
    in_specs = [pl.BlockSpec((1, ts, d), lambda i, j: (i, j + off, 0)),
                pl.BlockSpec((1, 6, d), lambda i, j: (i, 0, 0)),
                _resident((1, d)),
                _resident((d, 3 * d))]
    args = [x, mod, g.reshape(1, d), w_qkv]
    if emit_logf:
        in_specs += [_resident((d, LANES)), _resident((1, LANES))]
        args += [wf, bf]
    out_specs, out_shape = [], []
    if emit_qkv:
        out_specs += [row_spec(d)] * 3
        out_shape += [jax.ShapeDtypeStruct((b, rows, d), BF16)] * 3
    if emit_kv32:
        out_specs += [row_spec(d)] * 2
        out_shape += [jax.ShapeDtypeStruct((b, rows, d), F32)] * 2
    if emit_logf:
        out_specs += [row_spec(LANES)]
        out_shape += [jax.ShapeDtypeStruct((b, rows, LANES), F32)]
    return pl.pallas_call(
        functools.partial(_qkv_body, d=d, emit_qkv=emit_qkv, emit_kv32=emit_kv32,
                          emit_logf=emit_logf),
        grid=(b, rows // ts),
        in_specs=in_specs, out_specs=out_specs, out_shape=out_shape,
        compiler_params=_params("parallel", "parallel"),
        name="norm_qkv",
    )(*args)


def _oproj_body(o_ref, x_ref, mod_ref, w_ref, out_ref):
    y = jnp.dot(o_ref[0], w_ref[...], preferred_element_type=F32)
    out_ref[0] = x_ref[0] + mod_ref[0, 2:3, :] * y


def _out_proj(o, x, mod, w_o, *, ts):
    b, s, d = x.shape
    assert s % ts == 0
    row = pl.BlockSpec((1, ts, d), lambda i, j: (i, j, 0))
    return pl.pallas_call(
        _oproj_body,
        grid=(b, s // ts),
        in_specs=[row, row, pl.BlockSpec((1, 6, d), lambda i, j: (i, 0, 0)), _resident((d, d))],
        out_specs=row,
        out_shape=jax.ShapeDtypeStruct((b, s, d), F32),
        compiler_params=_params("parallel", "parallel"),
        name="out_proj",
    )(o, x, mod, w_o)


def _ffn_body(*refs, d_ff, final_norm):
    x_ref, mod_ref, g_ref, wgu_ref, wd_ref = refs[:5]
    out_ref = refs[-1]
    x = x_ref[0]
    h = _modulated_norm(x, g_ref[...], mod_ref[0, 4:5, :], mod_ref[0, 3:4, :])
    hb = h.astype(BF16)
    acc = jnp.zeros(x.shape, F32)
    for c in range(d_ff // FF_CHUNK):
        lo = c * FF_CHUNK
        gate = jnp.dot(hb, wgu_ref[:, lo:lo + FF_CHUNK], preferred_element_type=F32)
        up = jnp.dot(hb, wgu_ref[:, d_ff + lo:d_ff + lo + FF_CHUNK], preferred_element_type=F32)
        act = (jax.nn.silu(gate) * up).astype(BF16)
        acc = acc + jnp.dot(act, wd_ref[lo:lo + FF_CHUNK, :], preferred_element_type=F32)
    x2 = x + mod_ref[0, 5:6, :] * acc
    if final_norm:
        gout_ref = refs[5]
        ms = jnp.mean(x2 * x2, axis=-1, keepdims=True)
        x2 = x2 * lax.rsqrt(ms + RMS_EPS) * gout_ref[...]
    out_ref[0] = x2


def _ffn(x, mod, g, w_gu, w_down, g_out=None, *, ts):
    b, s, d = x.shape
    d_ff = w_down.shape[0]
    assert s % ts == 0 and d_ff % FF_CHUNK == 0
    row = pl.BlockSpec((1, ts, d), lambda i, j: (i, j, 0))
    in_specs = [row, pl.BlockSpec((1, 6, d), lambda i, j: (i, 0, 0)), _resident((1, d)),
                _resident((d, 2 * d_ff)), _resident((d_ff, d))]
    args = [x, mod, g.reshape(1, d), w_gu, w_down]
    if g_out is not None:
        in_specs.append(_resident((1, d)))
        args.append(g_out.reshape(1, d))
    return pl.pallas_call(
        functools.partial(_ffn_body, d_ff=d_ff, final_norm=g_out is not None),
        grid=(b, s // ts),
        in_specs=in_specs, out_specs=row,
        out_shape=jax.ShapeDtypeStruct((b, s, d), F32),
        compiler_params=_params("parallel", "parallel"),
        name="ffn",
    )(*args)


def _bias_row_body(tab_ref, g_ref):
    u = lax.broadcasted_iota(jnp.int32, (REL_PAD, G_WIDTH), 1)
    r = lax.broadcasted_iota(jnp.int32, (REL_PAD, G_WIDTH), 0)
    rel = A_PAST + (A_TILE - 1) - u
    idx = jnp.clip(rel, -(CHUNK - 1), MAX_REL) + (CHUNK - 1)
    onehot = jnp.where(r == idx, 1.0, 0.0).astype(BF16)
    hi, mid, lo = _split3(tab_ref[...])
    g_ref[...] = (jnp.dot(hi, onehot, preferred_element_type=F32)
                  + jnp.dot(mid, onehot, preferred_element_type=F32)
                  + jnp.dot(lo, onehot, preferred_element_type=F32))


def _bias_tile_body(g_ref, o_ref):
    x = jnp.broadcast_to(g_ref[0], (A_TILE, G_WIDTH))
    rolled = pltpu.roll(x, G_WIDTH - (A_TILE - 1), 1, stride=1, stride_axis=0)
    val = rolled[:, :A_WIN]
    qc = lax.broadcasted_iota(jnp.int32, (A_TILE, A_WIN), 0) // CHUNK
    kc = lax.broadcasted_iota(jnp.int32, (A_TILE, A_WIN), 1) // CHUNK
    band = (kc >= qc) & (kc <= qc + N_PAST_CHUNKS)
    o_ref[0] = jnp.where(band, val, NEG_INF)


def _rel_bias_tile(table):
    h, rel = table.shape
    assert rel == CHUNK + MAX_REL and rel <= REL_PAD and A_WIN + A_TILE - 1 <= G_WIDTH
    tab = jnp.pad(table, ((0, 0), (0, REL_PAD - rel)))
    gen = pl.pallas_call(
        _bias_row_body,
        out_shape=jax.ShapeDtypeStruct((h, G_WIDTH), F32),
        name="rel_bias_row",
    )(tab)
    return pl.pallas_call(
        _bias_tile_body,
        grid=(h,),
        in_specs=[pl.BlockSpec((1, 1, G_WIDTH), lambda i: (i, 0, 0))],
        out_specs=pl.BlockSpec((1, A_TILE, A_WIN), lambda i: (i, 0, 0)),
        out_shape=jax.ShapeDtypeStruct((h, A_TILE, A_WIN), F32),
        compiler_params=_params("parallel"),
        name="rel_bias_tile",
    )(gen.reshape(h, 1, G_WIDTH))


def _head_select(shape, hh):
    lane = lax.broadcasted_iota(jnp.int32, shape, len(shape) - 1)
    return (lane < HEAD_DIM) if hh == 0 else (lane >= HEAD_DIM)


def _softmax_pv(s, v):
    m = jnp.max(s, axis=-1, keepdims=True)
    p = jnp.exp(s - m)
    l = jnp.sum(p, axis=-1, keepdims=True)
    return jnp.dot(p.astype(BF16), v, preferred_element_type=F32), l


def _chunk_attn_body(q_ref, k0_ref, k1_ref, k2_ref, v0_ref, v1_ref, v2_ref, bias_ref, o_ref):
    t = pl.program_id(2)
    q = q_ref[0]
    kw = jnp.concatenate([k0_ref[0], k1_ref[0], k2_ref[0]], axis=0)
    vw = jnp.concatenate([v0_ref[0], v1_ref[0], v2_ref[0]], axis=0)
    col = lax.broadcasted_iota(jnp.int32, (1, A_WIN), 1)
    pen = jnp.where(col >= (2 - t) * A_TILE, 0.0, NEG_INF)
    outs = []
    for hh in range(HEADS_PER_BLOCK):
        qm = jnp.where(_head_select(q.shape, hh), q, jnp.zeros_like(q))
        s = _nt_dot(qm, kw) + bias_ref[hh] + pen
        o, l = _softmax_pv(s, vw)
        outs.append(o / l)
    o_ref[0] = jnp.where(_head_select(outs[0].shape, 0), outs[0], outs[1]).astype(o_ref.dtype)


def _chunk_attn_prompt(q, k, v, bias):
    b, s, d = q.shape
    assert A_WIN == 3 * A_TILE and s % A_TILE == 0 and d % LANES == 0
    pairs = d // LANES
    blk = lambda back: pl.BlockSpec(
        (1, A_TILE, LANES), lambda p, i, t: (i, jnp.maximum(t - back, 0), p))
    return pl.pallas_call(
        _chunk_attn_body,
        grid=(pairs, b, s // A_TILE),
        in_specs=[blk(0), blk(2), blk(1), blk(0), blk(2), blk(1), blk(0),
                  pl.BlockSpec((HEADS_PER_BLOCK, A_TILE, A_WIN), lambda p, i, t: (p, 0, 0))],
        out_specs=blk(0),
        out_shape=jax.ShapeDtypeStruct((b, s, d), BF16),
        compiler_params=_params("parallel", "parallel", "parallel"),
        name="chunk_attn",
    )(q, k, k, k, v, v, v, bias)


def _chunk_attn_sample_body(q_ref, kc_ref, vc_ref, kn_ref, vn_ref, bias_ref, o_ref, *, n_cache):
    q = q_ref[0]
    t = q.shape[0]
    kc = kc_ref[0, 0].astype(BF16)
    vc = vc_ref[0, 0].astype(BF16)
    kn = kn_ref[0]
    vn = vn_ref[0]
    outs = []
    for hh in range(HEADS_PER_BLOCK):
        qm = jnp.where(_head_select(q.shape, hh), q, jnp.zeros_like(q))
        sc = _nt_dot(qm, kc) + bias_ref[hh, :, 0:n_cache]
        sn = _nt_dot(qm, kn) + bias_ref[hh, :, n_cache:n_cache + t]
        m = jnp.maximum(jnp.max(sc, axis=-1, keepdims=True), jnp.max(sn, axis=-1, keepdims=True))
        pc = jnp.exp(sc - m)
        pn = jnp.exp(sn - m)
        l = jnp.sum(pc, axis=-1, keepdims=True) + jnp.sum(pn, axis=-1, keepdims=True)
        o = (jnp.dot(pc.astype(BF16), vc, preferred_element_type=F32)
             + jnp.dot(pn.astype(BF16), vn, preferred_element_type=F32))
        outs.append(o / l)
    o_ref[0] = jnp.where(_head_select(outs[0].shape, 0), outs[0], outs[1]).astype(o_ref.dtype)


def _chunk_attn_sample(q, k, v, k_cache, v_cache, bias, layer):
    b, t, d = q.shape
    n_cache = k_cache.shape[2]
    assert n_cache == A_PAST and t <= CHUNK and n_cache + t <= A_WIN
    pairs = d // LANES
    new = pl.BlockSpec((1, t, LANES), lambda i, p: (i, 0, p))
    cache = pl.BlockSpec((1, 1, n_cache, LANES), lambda i, p: (layer, i, 0, p))
    return pl.pallas_call(
        functools.partial(_chunk_attn_sample_body, n_cache=n_cache),
        grid=(b, pairs),
        in_specs=[new, cache, cache, new, new,
                  pl.BlockSpec((HEADS_PER_BLOCK, t, A_WIN), lambda i, p: (p, 0, 0))],
        out_specs=new,
        out_shape=jax.ShapeDtypeStruct((b, t, d), BF16),
        compiler_params=_params("parallel", "parallel"),
        name="chunk_attn_sample",
    )(q, k_cache, v_cache, k, v, bias)


def _cumsum_body(x_ref, fcol_ref, frow_ref, carry_ref, *, tc, n_heads):
    @pl.when(pl.program_id(1) == 0)
    def _():
        carry_ref[...] = jnp.zeros_like(carry_ref)

    r = lax.broadcasted_iota(jnp.int32, (tc, tc), 0)
    c = lax.broadcasted_iota(jnp.int32, (tc, tc), 1)
    tri = jnp.where(c <= r, 1.0, 0.0).astype(BF16)
    hi, mid, lo = _split3(x_ref[0])
    f = (jnp.dot(tri, hi, preferred_element_type=F32)
         + jnp.dot(tri, mid, preferred_element_type=F32)
         + jnp.dot(tri, lo, preferred_element_type=F32)) + carry_ref[0:1, :]
    fcol_ref[0] = f
    frow_ref[0] = f.T[0:n_heads, :]
    carry_ref[0:1, :] = f[tc - 1:tc, :]


def _cumsum(logf, *, tc, n_heads=N_HEADS):
    b, s, w = logf.shape
    assert s % tc == 0 and tc % LANES == 0 and w == LANES
    return pl.pallas_call(
        functools.partial(_cumsum_body, tc=tc, n_heads=n_heads),
        grid=(b, s // tc),
        in_specs=[pl.BlockSpec((1, tc, LANES), lambda i, j: (i, j, 0))],
        out_specs=[pl.BlockSpec((1, tc, LANES), lambda i, j: (i, j, 0)),
                   pl.BlockSpec((1, n_heads, tc), lambda i, j: (i, 0, j))],
        out_shape=[jax.ShapeDtypeStruct((b, s, LANES), F32),
                   jax.ShapeDtypeStruct((b, n_heads, s), F32)],
        scratch_shapes=[pltpu.VMEM((8, LANES), F32)],
        compiler_params=_params("parallel", "arbitrary"),
        name="logf_cumsum",
    )(logf)


def _head_column(f_blk, head):
    lane = lax.broadcasted_iota(jnp.int32, f_blk.shape, 1)
    return jnp.sum(jnp.where(lane == head, f_blk, 0.0), axis=1, keepdims=True)


def _fox_body(q_ref, k_ref, v_ref, fq_ref, fk_ref, o_ref, qm_ref, fqc_ref, m_ref, l_ref, acc_ref,
              *, tile):
    p = pl.program_id(1)
    i = pl.program_id(2)
    q = q_ref[0]
    for hh in range(HEADS_PER_BLOCK):
        qm_ref[hh] = jnp.where(_head_select(q.shape, hh), q, jnp.zeros_like(q))
        fqc_ref[hh] = _head_column(fq_ref[0], HEADS_PER_BLOCK * p + hh)
        m_ref[hh] = jnp.full(m_ref.shape[1:], NEG_INF, F32)
        l_ref[hh] = jnp.zeros(l_ref.shape[1:], F32)
        acc_ref[hh] = jnp.zeros(acc_ref.shape[1:], F32)

    def block(j, masked):
        off = pl.multiple_of(j * tile, tile)
        kb = k_ref[0, pl.ds(off, tile), :]
        vb = v_ref[0, pl.ds(off, tile), :]
        for hh in range(HEADS_PER_BLOCK):
            fk = fk_ref[0, 0, hh:hh + 1, pl.ds(off, tile)]
            s = _nt_dot(qm_ref[hh], kb) + (fqc_ref[hh] - fk)
            if masked:
                row = lax.broadcasted_iota(jnp.int32, s.shape, 0)
                col = lax.broadcasted_iota(jnp.int32, s.shape, 1)
                s = jnp.where(col <= row, s, NEG_INF)
            m_prev = m_ref[hh]
            m_new = jnp.maximum(m_prev, jnp.max(s, axis=-1, keepdims=True))
            alpha = jnp.exp(m_prev - m_new)
            pr = jnp.exp(s - m_new)
            l_ref[hh] = alpha * l_ref[hh] + jnp.sum(pr, axis=-1, keepdims=True)
            acc_ref[hh] = alpha * acc_ref[hh] + jnp.dot(pr.astype(BF16), vb,
                                                         preferred_element_type=F32)
            m_ref[hh] = m_new

    def full_block(j, carry):
        block(j, masked=False)
        return carry

    lax.fori_loop(0, i, full_block, 0)
    block(i, masked=True)
    o0 = acc_ref[0] / l_ref[0]
    o1 = acc_ref[1] / l_ref[1]
    o_ref[0] = jnp.where(_head_select(o0.shape, 0), o0, o1).astype(o_ref.dtype)


def _fox_attn_prompt(q, k, v, f_col, f_row):
    b, s, d = q.shape
    tile = FOX_TILE
    assert s % tile == 0
    pairs = d // LANES
    n_heads = f_row.shape[1]
    f_row = f_row.reshape(b, pairs, HEADS_PER_BLOCK, s)
    qblk = pl.BlockSpec((1, tile, LANES), lambda i, p, t: (i, t, p))
    seq = pl.BlockSpec((1, s, LANES), lambda i, p, t: (i, 0, p))
    return pl.pallas_call(
        functools.partial(_fox_body, tile=tile),
        grid=(b, pairs, s // tile),
        in_specs=[qblk, seq, seq,
                  pl.BlockSpec((1, tile, LANES), lambda i, p, t: (i, t, 0)),
                  pl.BlockSpec((1, 1, HEADS_PER_BLOCK, s), lambda i, p, t: (i, p, 0, 0))],
        out_specs=qblk,
        out_shape=jax.ShapeDtypeStruct((b, s, d), BF16),
        scratch_shapes=[pltpu.VMEM((HEADS_PER_BLOCK, tile, LANES), BF16),
                        pltpu.VMEM((HEADS_PER_BLOCK, tile, 1), F32),
                        pltpu.VMEM((HEADS_PER_BLOCK, tile, 1), F32),
                        pltpu.VMEM((HEADS_PER_BLOCK, tile, 1), F32),
                        pltpu.VMEM((HEADS_PER_BLOCK, tile, LANES), F32)],
        compiler_params=_params("parallel", "parallel", "parallel"),
        name="fox_attn",
    )(q, k, v, f_col, f_row)


def _fox_sample_body(q_ref, kc_ref, vc_ref, kn_ref, vn_ref, fq_ref, fk_ref, o_ref, *, n_cache):
    p = pl.program_id(1)
    q = q_ref[0]
    t = q.shape[0]
    kc = kc_ref[0, 0].astype(BF16)
    vc = vc_ref[0, 0].astype(BF16)
    kn = kn_ref[0]
    vn = vn_ref[0]
    row = lax.broadcasted_iota(jnp.int32, (t, t), 0)
    col = lax.broadcasted_iota(jnp.int32, (t, t), 1)
    outs = []
    for hh in range(HEADS_PER_BLOCK):
        qm = jnp.where(_head_select(q.shape, hh), q, jnp.zeros_like(q))
        fq = _head_column(fq_ref[0], HEADS_PER_BLOCK * p + hh)
        sc = _nt_dot(qm, kc) + (fq - fk_ref[0, 0, hh:hh + 1, 0:n_cache])
        sn = _nt_dot(qm, kn) + (fq - fk_ref[0, 0, hh:hh + 1, n_cache:n_cache + t])
        sn = jnp.where(col <= row, sn, NEG_INF)
        m = jnp.maximum(jnp.max(sc, axis=-1, keepdims=True), jnp.max(sn, axis=-1, keepdims=True))
        pc = jnp.exp(sc - m)
        pn = jnp.exp(sn - m)
        l = jnp.sum(pc, axis=-1, keepdims=True) + jnp.sum(pn, axis=-1, keepdims=True)
        o = (jnp.dot(pc.astype(BF16), vc, preferred_element_type=F32)
             + jnp.dot(pn.astype(BF16), vn, preferred_element_type=F32))
        outs.append(o / l)
    o_ref[0] = jnp.where(_head_select(outs[0].shape, 0), outs[0], outs[1]).astype(o_ref.dtype)


def _fox_attn_sample(q, k, v, k_cache, v_cache, f_col_new, f_row, layer):
    b, t, d = q.shape
    n_cache = k_cache.shape[2]
    pairs = d // LANES
    width = f_row.shape[2]
    f_row = f_row.reshape(b, pairs, HEADS_PER_BLOCK, width)
    new = pl.BlockSpec((1, t, LANES), lambda i, p: (i, 0, p))
    cache = pl.BlockSpec((1, 1, n_cache, LANES), lambda i, p: (layer, i, 0, p))
    return pl.pallas_call(
        functools.partial(_fox_sample_body, n_cache=n_cache),
        grid=(b, pairs),
        in_specs=[new, cache, cache, new, new,
                  pl.BlockSpec((1, t, LANES), lambda i, p: (i, 0, 0)),
                  pl.BlockSpec((1, 1, HEADS_PER_BLOCK, width), lambda i, p: (i, p, 0, 0))],
        out_specs=new,
        out_shape=jax.ShapeDtypeStruct((b, t, d), BF16),
        compiler_params=_params("parallel", "parallel"),
        name="fox_attn_sample",
    )(q, k_cache, v_cache, k, v, f_col_new, f_row)


def _trunk(x, mods, weights, a_mix, b_mix, *, ts, keep):
    b, s, d = x.shape
    depth = mods.shape[0]
    a_state, b_state = [], []
    for l in range(depth):
        mod = mods[l]
        if l % 2 == 0 and keep == s:
            q, k, v, k32, v32 = _qkv_proj(x, mod, weights["g_mix"][l], weights["w_qkv"][l],
                                          ts=ts, emit_kv32=True)
            a_state.append((k32.reshape(b, keep, N_HEADS, HEAD_DIM),
                            v32.reshape(b, keep, N_HEADS, HEAD_DIM)))
            o = a_mix(l // 2, q, k, v)
        elif l % 2 == 0:
            q, k, v = _qkv_proj(x, mod, weights["g_mix"][l], weights["w_qkv"][l], ts=ts)
            k32, v32 = _qkv_proj(x, mod, weights["g_mix"][l], weights["w_qkv"][l],
                                 ts=min(ts, keep), emit_qkv=False, emit_kv32=True,
                                 row_start=s - keep, rows=keep)
            a_state.append((k32.reshape(b, keep, N_HEADS, HEAD_DIM),
                            v32.reshape(b, keep, N_HEADS, HEAD_DIM)))
            o = a_mix(l // 2, q, k, v)
        else:
            ib = l // 2
            q, k, v, k32, v32, logf = _qkv_proj(
                x, mod, weights["g_mix"][l], weights["w_qkv"][l],
                weights["w_fgate"][ib], weights["b_fgate"][ib], ts=ts, emit_kv32=True)
            b_state.append((k32.reshape(b, s, N_HEADS, HEAD_DIM),
                            v32.reshape(b, s, N_HEADS, HEAD_DIM), logf[:, :, :N_HEADS]))
            o = b_mix(ib, q, k, v, logf)
        x = _out_proj(o, x, mod, weights["w_o"][l], ts=ts)
        g_out = weights["g_out"] if l == depth - 1 else None
        x = _ffn(x, mod, weights["g_ffn"][l], weights["w_gu"][l], weights["w_down"][l], g_out, ts=ts)
    return x, a_state, b_state


def kernel(x_prompt, x_sample, cache_a_k, cache_a_v, cache_b_k, cache_b_v, cache_b_logf,
           c_prompt, c_sample, w_mod, b_mod, g_mix, g_ffn, w_qkv, w_o, rel_bias,
           w_fgate, b_fgate, w_gu, w_down, g_out):
    bp, sp, d = x_prompt.shape
    bs, t_new, _ = x_sample.shape
    depth = w_mod.shape[0]
    n_b = w_fgate.shape[0]
    assert d == N_HEADS * HEAD_DIM
    keep = min(A_PAST, sp)

    weights = {
        "g_mix": g_mix, "g_ffn": g_ffn, "g_out": g_out,
        "w_qkv": w_qkv.astype(BF16), "w_o": w_o.astype(BF16),
        "w_gu": w_gu.astype(BF16), "w_down": w_down.astype(BF16),
        "w_fgate": jnp.pad(w_fgate, ((0, 0), (0, 0), (0, LANES - N_HEADS))).astype(BF16),
        "b_fgate": jnp.pad(b_fgate, ((0, 0), (0, LANES - N_HEADS))).reshape(n_b, 1, LANES),
    }

    c_all = jnp.concatenate([c_prompt, c_sample], axis=0)
    rows = -(-c_all.shape[0] // 16) * 16
    c_all = jnp.pad(c_all, ((0, rows - c_all.shape[0]), (0, 0)))
    mods = _modulation(c_all, w_mod, b_mod)
    mods_p = mods[:, :bp].reshape(depth, bp, 6, d)
    mods_s = mods[:, bp:bp + bs].reshape(depth, bs, 6, d)

    bias = [_rel_bias_tile(rel_bias[i]) for i in range(rel_bias.shape[0])]

    def a_prompt(i, q, k, v):
        return _chunk_attn_prompt(q, k, v, bias[i])

    def b_prompt(i, q, k, v, logf):
        f_col, f_row = _cumsum(logf, tc=FOX_TILE)
        return _fox_attn_prompt(q, k, v, f_col, f_row)

    y_prompt, pa, pb = _trunk(x_prompt, mods_p, weights, a_prompt, b_prompt, ts=ROW_TILE, keep=keep)

    n_a = cache_a_k.shape[0]
    ca_k = cache_a_k.reshape(n_a, bs, cache_a_k.shape[2], d)
    ca_v = cache_a_v.reshape(n_a, bs, cache_a_v.shape[2], d)
    past = cache_b_k.shape[2]
    cb_k = cache_b_k.reshape(n_b, bs, past, d)
    cb_v = cache_b_v.reshape(n_b, bs, past, d)
    tc_s = 384
    total = -(-(past + t_new) // tc_s) * tc_s

    def a_sample(i, q, k, v):
        return _chunk_attn_sample(q, k, v, ca_k, ca_v, bias[i], i)

    def b_sample(i, q, k, v, logf):
        lf_cache = jnp.pad(cache_b_logf[i].astype(F32), ((0, 0), (0, 0), (0, LANES - N_HEADS)))
        lf = jnp.concatenate([lf_cache, logf], axis=1)
        lf = jnp.pad(lf, ((0, 0), (0, total - (past + t_new)), (0, 0)))
        f_col, f_row = _cumsum(lf, tc=tc_s)
        return _fox_attn_sample(q, k, v, cb_k, cb_v, f_col[:, past:past + t_new], f_row, i)

    y_sample, sa, sb = _trunk(x_sample, mods_s, weights, a_sample, b_sample, ts=t_new, keep=t_new)

    stack = lambda states, j: jnp.stack([st[j] for st in states])
    return (y_prompt, y_sample, stack(pa, 0), stack(pa, 1), stack(pb, 0), stack(pb, 1), stack(pb, 2),
            stack(sa, 0), stack(sa, 1), stack(sb, 0), stack(sb, 1), stack(sb, 2))
```

```python
import functools

import jax
import jax.numpy as jnp
from jax import lax
from jax.experimental import pallas as pl
from jax.experimental.pallas import tpu as pltpu

F32 = jnp.float32
BF16 = jnp.bfloat16

N_HEADS = 16
HEAD_DIM = 64
CHUNK = 64
N_PAST_CHUNKS = 8
A_PAST = N_PAST_CHUNKS * CHUNK
MAX_REL = 256
RMS_EPS = 1e-6
NEG_INF = -1e30
ATTN_SCALE = HEAD_DIM ** -0.5

LANES = 128
HEADS_PER_BLOCK = LANES // HEAD_DIM
VMEM_LIMIT_BYTES = 48 * 1024 * 1024

ROW_TILE = 512
A_TILE = 256
A_WIN = A_TILE + A_PAST
FOX_TILE = 512
FOX_QUERY_SPLIT = 2
FF_CHUNK = 256
REL_PAD = 384
G_WIDTH = 1024
SAMPLE_CUMSUM_TILE = 384


def _params(*sem):
    return pltpu.CompilerParams(dimension_semantics=sem, vmem_limit_bytes=VMEM_LIMIT_BYTES)


def _resident(shape):
    zeros = (0,) * len(shape)
    return pl.BlockSpec(shape, lambda *_: zeros, pipeline_mode=pl.Buffered(1))


def _nt_dot(a, b):
    return lax.dot_general(a, b, (((1,), (1,)), ((), ())), preferred_element_type=F32)


def _modulated_norm(x, g, scale, shift):
    ms = jnp.mean(x * x, axis=-1, keepdims=True)
    y = x * lax.rsqrt(ms + RMS_EPS) * g
    return y * (1.0 + scale) + shift


def _split3(x):
    hi = x.astype(BF16)
    r1 = x - hi.astype(F32)
    mid = r1.astype(BF16)
    lo = (r1 - mid.astype(F32)).astype(BF16)
    return hi, mid, lo


def _lower_lanes(shape):
    return lax.broadcasted_iota(jnp.int32, shape, len(shape) - 1) < HEAD_DIM


def _mod_body(c_ref, w_ref, b_ref, o_ref):
    cs = jax.nn.silu(c_ref[...]).astype(BF16)
    o_ref[0] = jnp.dot(cs, w_ref[0].astype(BF16), preferred_element_type=F32) + b_ref[0]


def _modulation(c, w_mod, b_mod):
    depth, d, n = w_mod.shape
    rows = c.shape[0]
    tn = 1536
    assert n % tn == 0 and rows % 16 == 0
    return pl.pallas_call(
        _mod_body,
        grid=(depth, n // tn),
        in_specs=[pl.BlockSpec((rows, d), lambda l, j: (0, 0)),
                  pl.BlockSpec((1, d, tn), lambda l, j: (l, 0, j)),
                  pl.BlockSpec((1, 1, tn), lambda l, j: (l, 0, j))],
        out_specs=pl.BlockSpec((1, rows, tn), lambda l, j: (l, 0, j)),
        out_shape=jax.ShapeDtypeStruct((depth, rows, n), F32),
        compiler_params=_params("parallel", "parallel"),
        name="modulation",
    )(c, w_mod, b_mod.reshape(depth, 1, n))


def _qkv_body(*refs, d, emit):
    x_ref, mod_ref, g_ref, w_ref = refs[:4]
    pos = 4
    if "logf" in emit:
        wf_ref, bf_ref = refs[4:6]
        pos = 6
    out = dict(zip(emit, refs[pos:]))
    h = _modulated_norm(x_ref[0], g_ref[...], mod_ref[0, 1:2, :], mod_ref[0, 0:1, :])
    hb = h.astype(BF16)
    if "q" in emit:
        q = jnp.dot(hb, w_ref[:, 0:d], preferred_element_type=F32)
        out["q"][0] = (q * ATTN_SCALE).astype(BF16)
    k = jnp.dot(hb, w_ref[:, d:2 * d], preferred_element_type=F32)
    v = jnp.dot(hb, w_ref[:, 2 * d:3 * d], preferred_element_type=F32)
    if "k" in emit:
        out["k"][0] = k.astype(BF16)
    if "v" in emit:
        out["v"][0] = v.astype(BF16)
    if "vt" in emit:
        out["vt"][0] = v.T.astype(BF16)
    if "k32" in emit:
        out["k32"][0] = k
    if "v32" in emit:
        out["v32"][0] = v
    if "logf" in emit:
        z = jnp.dot(hb, wf_ref[...], preferred_element_type=F32) + bf_ref[...]
        out["logf"][0] = jax.nn.log_sigmoid(z)


def _qkv_proj(x, mod, g, w_qkv, wf=None, bf=None, *, ts, emit, row_start=0, rows=None):
    b, s, d = x.shape
    rows = s if rows is None else rows
    assert rows % ts == 0 and row_start % ts == 0
    off = row_start // ts
    assert ("logf" in emit) == (wf is not None)
    row_spec = lambda width: pl.BlockSpec((1, ts, width), lambda i, j: (i, j, 0))
    in_specs = [pl.BlockSpec((1, ts, d), lambda i, j: (i, j + off, 0)),
                pl.BlockSpec((1, 6, d), lambda i, j: (i, 0, 0)),
                _resident((1, d)),
                _resident((d, 3 * d))]
    args = [x, mod, g.reshape(1, d), w_qkv]
    if wf is not None:
        in_specs += [_resident((d, LANES)), _resident((1, LANES))]
        args += [wf, bf]
    kinds = {
        "q": (row_spec(d), (b, rows, d), BF16), "k": (row_spec(d), (b, rows, d), BF16),
        "v": (row_spec(d), (b, rows, d), BF16),
        "vt": (pl.BlockSpec((1, d, ts), lambda i, j: (i, 0, j)), (b, d, rows), BF16),
        "k32": (row_spec(d), (b, rows, d), F32), "v32": (row_spec(d), (b, rows, d), F32),
        "logf": (row_spec(LANES), (b, rows, LANES), F32),
    }
    outs = pl.pallas_call(
        functools.partial(_qkv_body, d=d, emit=emit),
        grid=(b, rows // ts),
        in_specs=in_specs,
        out_specs=[kinds[e][0] for e in emit],
        out_shape=[jax.ShapeDtypeStruct(kinds[e][1], kinds[e][2]) for e in emit],
        compiler_params=_params("parallel", "parallel"),
        name="norm_qkv",
    )(*args)
    return dict(zip(emit, outs))


def _oproj_body(o_ref, x_ref, mod_ref, w_ref, out_ref):
    y = jnp.dot(o_ref[0], w_ref[...], preferred_element_type=F32)
    out_ref[0] = x_ref[0] + mod_ref[0, 2:3, :] * y


def _out_proj(o, x, mod, w_o, *, ts):
    b, s, d = x.shape
    assert s % ts == 0
    row = pl.BlockSpec((1, ts, d), lambda i, j: (i, j, 0))
    return pl.pallas_call(
        _oproj_body,
        grid=(b, s // ts),
        in_specs=[row, row, pl.BlockSpec((1, 6, d), lambda i, j: (i, 0, 0)), _resident((d, d))],
        out_specs=row,
        out_shape=jax.ShapeDtypeStruct((b, s, d), F32),
        compiler_params=_params("parallel", "parallel"),
        name="out_proj",
    )(o, x, mod, w_o)


def _ffn_body(*refs, d_ff, final_norm):
    x_ref, mod_ref, g_ref, wgu_ref, wd_ref = refs[:5]
    out_ref = refs[-1]
    x = x_ref[0]
    h = _modulated_norm(x, g_ref[...], mod_ref[0, 4:5, :], mod_ref[0, 3:4, :])
    hb = h.astype(BF16)
    acc = jnp.zeros(x.shape, F32)
    for c in range(d_ff // FF_CHUNK):
        lo = c * FF_CHUNK
        gate = jnp.dot(hb, wgu_ref[:, lo:lo + FF_CHUNK], preferred_element_type=F32)
        up = jnp.dot(hb, wgu_ref[:, d_ff + lo:d_ff + lo + FF_CHUNK], preferred_element_type=F32)
        act = (jax.nn.silu(gate) * up).astype(BF16)
        acc = acc + jnp.dot(act, wd_ref[lo:lo + FF_CHUNK, :], preferred_element_type=F32)
    x2 = x + mod_ref[0, 5:6, :] * acc
    if final_norm:
        gout_ref = refs[5]
        ms = jnp.mean(x2 * x2, axis=-1, keepdims=True)
        x2 = x2 * lax.rsqrt(ms + RMS_EPS) * gout_ref[...]
    out_ref[0] = x2


def _ffn(x, mod, g, w_gu, w_down, g_out=None, *, ts):
    b, s, d = x.shape
    d_ff = w_down.shape[0]
    assert s % ts == 0 and d_ff % FF_CHUNK == 0
    row = pl.BlockSpec((1, ts, d), lambda i, j: (i, j, 0))
    in_specs = [row, pl.BlockSpec((1, 6, d), lambda i, j: (i, 0, 0)), _resident((1, d)),
                _resident((d, 2 * d_ff)), _resident((d_ff, d))]
    args = [x, mod, g.reshape(1, d), w_gu, w_down]
    if g_out is not None:
        in_specs.append(_resident((1, d)))
        args.append(g_out.reshape(1, d))
    return pl.pallas_call(
        functools.partial(_ffn_body, d_ff=d_ff, final_norm=g_out is not None),
        grid=(b, s // ts),
        in_specs=in_specs, out_specs=row,
        out_shape=jax.ShapeDtypeStruct((b, s, d), F32),
        compiler_params=_params("parallel", "parallel"),
        name="ffn",
    )(*args)


def _bias_row_body(tab_ref, g_ref):
    u = lax.broadcasted_iota(jnp.int32, (REL_PAD, G_WIDTH), 1)
    r = lax.broadcasted_iota(jnp.int32, (REL_PAD, G_WIDTH), 0)
    rel = A_PAST + (A_TILE - 1) - u
    idx = jnp.clip(rel, -(CHUNK - 1), MAX_REL) + (CHUNK - 1)
    onehot = jnp.where(r == idx, 1.0, 0.0).astype(BF16)
    hi, mid, lo = _split3(tab_ref[...])
    g_ref[...] = (jnp.dot(hi, onehot, preferred_element_type=F32)
                  + jnp.dot(mid, onehot, preferred_element_type=F32)
                  + jnp.dot(lo, onehot, preferred_element_type=F32))


def _bias_tile_body(g_ref, o_ref, ot_ref):
    x = jnp.broadcast_to(g_ref[0], (A_TILE, G_WIDTH))
    rolled = pltpu.roll(x, G_WIDTH - (A_TILE - 1), 1, stride=1, stride_axis=0)
    val = rolled[:, :A_WIN]
    qc = lax.broadcasted_iota(jnp.int32, (A_TILE, A_WIN), 0) // CHUNK
    kc = lax.broadcasted_iota(jnp.int32, (A_TILE, A_WIN), 1) // CHUNK
    band = (kc >= qc) & (kc <= qc + N_PAST_CHUNKS)
    tile = jnp.where(band, val, NEG_INF)
    o_ref[0] = tile
    ot_ref[0] = tile.T


def _rel_bias_tile(table):
    h, rel = table.shape
    assert rel == CHUNK + MAX_REL and rel <= REL_PAD and A_WIN + A_TILE - 1 <= G_WIDTH
    tab = jnp.pad(table, ((0, 0), (0, REL_PAD - rel)))
    gen = pl.pallas_call(
        _bias_row_body,
        out_shape=jax.ShapeDtypeStruct((h, G_WIDTH), F32),
        name="rel_bias_row",
    )(tab)
    return pl.pallas_call(
        _bias_tile_body,
        grid=(h,),
        in_specs=[pl.BlockSpec((1, 1, G_WIDTH), lambda i: (i, 0, 0))],
        out_specs=[pl.BlockSpec((1, A_TILE, A_WIN), lambda i: (i, 0, 0)),
                   pl.BlockSpec((1, A_WIN, A_TILE), lambda i: (i, 0, 0))],
        out_shape=[jax.ShapeDtypeStruct((h, A_TILE, A_WIN), F32),
                   jax.ShapeDtypeStruct((h, A_WIN, A_TILE), F32)],
        compiler_params=_params("parallel"),
        name="rel_bias_tile",
    )(gen.reshape(h, 1, G_WIDTH))


def _head_select(shape, hh):
    lower = _lower_lanes(shape)
    return lower if hh == 0 else jnp.logical_not(lower)


def _chunk_attn_body(q_ref, k0_ref, k1_ref, k2_ref, v0_ref, v1_ref, v2_ref, bias_ref, o_ref):
    t = pl.program_id(2)
    q = q_ref[0]
    kw = jnp.concatenate([k0_ref[0], k1_ref[0], k2_ref[0]], axis=0)
    vt = jnp.concatenate([v0_ref[0], v1_ref[0], v2_ref[0]], axis=1)
    ones = jnp.ones((HEAD_DIM, A_WIN), BF16)
    vh = (jnp.concatenate([vt[:HEAD_DIM], ones], axis=0),
          jnp.concatenate([ones, vt[HEAD_DIM:]], axis=0))
    key = lax.broadcasted_iota(jnp.int32, kw.shape, 0)
    pen = jnp.where(key >= (2 - t) * A_TILE, 0.0, NEG_INF).astype(BF16)
    k_lower = _lower_lanes(kw.shape)
    kh = (jnp.where(k_lower, kw, pen), jnp.where(k_lower, pen, kw))
    q_lane = lax.broadcasted_iota(jnp.int32, q.shape, 1)
    q_lower = _lower_lanes(q.shape)
    qh = (jnp.where(q_lower, q, jnp.where(q_lane == HEAD_DIM, 1.0, 0.0).astype(BF16)),
          jnp.where(q_lower, jnp.where(q_lane == 0, 1.0, 0.0).astype(BF16), q))
    st = [_nt_dot(kh[hh], qh[hh]) + bias_ref[hh] for hh in range(HEADS_PER_BLOCK)]
    outs = []
    for hh in range(HEADS_PER_BLOCK):
        m = jnp.max(st[hh], axis=0, keepdims=True)
        pt = jnp.exp(st[hh] - m).astype(BF16)
        acc = jnp.dot(vh[hh], pt, preferred_element_type=F32)
        outs.append(acc[:HEAD_DIM] / acc[HEAD_DIM:] if hh == 0
                    else acc[HEAD_DIM:] / acc[:HEAD_DIM])
    o_ref[0] = jnp.concatenate(outs, axis=0).T.astype(o_ref.dtype)


def _chunk_attn_prompt(q, k, vt, bias_t):
    b, s, d = q.shape
    assert A_WIN == 3 * A_TILE and s % A_TILE == 0 and d % LANES == 0
    pairs = d // LANES
    rows = lambda back: pl.BlockSpec(
        (1, A_TILE, LANES), lambda p, i, t: (i, jnp.maximum(t - back, 0), p))
    cols = lambda back: pl.BlockSpec(
        (1, LANES, A_TILE), lambda p, i, t: (i, p, jnp.maximum(t - back, 0)))
    return pl.pallas_call(
        _chunk_attn_body,
        grid=(pairs, b, s // A_TILE),
        in_specs=[rows(0), rows(2), rows(1), rows(0), cols(2), cols(1), cols(0),
                  pl.BlockSpec((HEADS_PER_BLOCK, A_WIN, A_TILE), lambda p, i, t: (p, 0, 0))],
        out_specs=rows(0),
        out_shape=jax.ShapeDtypeStruct((b, s, d), BF16),
        compiler_params=_params("parallel", "parallel", "parallel"),
        name="chunk_attn",
    )(q, k, k, k, vt, vt, vt, bias_t)


def _chunk_attn_sample_body(q_ref, kc_ref, vc_ref, kn_ref, vn_ref, bias_ref, o_ref, *, n_cache):
    q = q_ref[0]
    t = q.shape[0]
    kc = kc_ref[0, 0].astype(BF16)
    vc = vc_ref[0, 0].astype(BF16)
    kn = kn_ref[0]
    vn = vn_ref[0]
    outs = []
    for hh in range(HEADS_PER_BLOCK):
        qm = jnp.where(_head_select(q.shape, hh), q, jnp.zeros_like(q))
        sc = _nt_dot(qm, kc) + bias_ref[hh, :, 0:n_cache]
        sn = _nt_dot(qm, kn) + bias_ref[hh, :, n_cache:n_cache + t]
        m = jnp.maximum(jnp.max(sc, axis=-1, keepdims=True), jnp.max(sn, axis=-1, keepdims=True))
        pc = jnp.exp(sc - m)
        pn = jnp.exp(sn - m)
        l = jnp.sum(pc, axis=-1, keepdims=True) + jnp.sum(pn, axis=-1, keepdims=True)
        o = (jnp.dot(pc.astype(BF16), vc, preferred_element_type=F32)
             + jnp.dot(pn.astype(BF16), vn, preferred_element_type=F32))
        outs.append(o / l)
    o_ref[0] = jnp.where(_lower_lanes(outs[0].shape), outs[0], outs[1]).astype(o_ref.dtype)


def _chunk_attn_sample(q, k, v, k_cache, v_cache, bias, layer):
    b, t, d = q.shape
    n_cache = k_cache.shape[2]
    assert n_cache == A_PAST and t <= CHUNK and n_cache + t <= A_WIN
    pairs = d // LANES
    new = pl.BlockSpec((1, t, LANES), lambda i, p: (i, 0, p))
    cache = pl.BlockSpec((1, 1, n_cache, LANES), lambda i, p: (layer, i, 0, p))
    return pl.pallas_call(
        functools.partial(_chunk_attn_sample_body, n_cache=n_cache),
        grid=(b, pairs),
        in_specs=[new, cache, cache, new, new,
                  pl.BlockSpec((HEADS_PER_BLOCK, t, A_WIN), lambda i, p: (p, 0, 0))],
        out_specs=new,
        out_shape=jax.ShapeDtypeStruct((b, t, d), BF16),
        compiler_params=_params("parallel", "parallel"),
        name="chunk_attn_sample",
    )(q, k_cache, v_cache, k, v, bias)


def _bias_lanes(f, d):
    hi, mid, lo = _split3(f)
    col = lax.broadcasted_iota(jnp.int32, (LANES, d), 1)
    row = lax.broadcasted_iota(jnp.int32, (LANES, d), 0)
    lane = col % LANES
    head = HEADS_PER_BLOCK * (col // LANES) + jnp.where(lane >= HEAD_DIM, 0, 1)
    slot = lane % HEAD_DIM
    hit = row == head

    def pick(first):
        out = None
        for n, piece in enumerate((hi, mid, lo)):
            sel = jnp.where(hit & (slot == first + n), 1.0, 0.0).astype(BF16)
            term = jnp.dot(piece, sel, preferred_element_type=F32)
            out = term if out is None else out + term
        return out

    slot_row = lax.broadcasted_iota(jnp.int32, (1, d), 1) % HEAD_DIM
    ones = lambda first: jnp.where((slot_row >= first) & (slot_row < first + 3), 1.0, 0.0)
    q_side = pick(0) + ones(3)
    k_side = ones(0) - pick(3)
    return q_side.astype(BF16), k_side.astype(BF16)


def _cumsum_body(x_ref, *refs, tc, n_heads, d, emit_bias_lanes):
    carry_ref = refs[-1]

    @pl.when(pl.program_id(1) == 0)
    def _():
        carry_ref[...] = jnp.zeros_like(carry_ref)

    r = lax.broadcasted_iota(jnp.int32, (tc, tc), 0)
    c = lax.broadcasted_iota(jnp.int32, (tc, tc), 1)
    tri = jnp.where(c <= r, 1.0, 0.0).astype(BF16)
    hi, mid, lo = _split3(x_ref[0])
    f = (jnp.dot(tri, hi, preferred_element_type=F32)
         + jnp.dot(tri, mid, preferred_element_type=F32)
         + jnp.dot(tri, lo, preferred_element_type=F32)) + carry_ref[0:1, :]
    carry_ref[0:1, :] = f[tc - 1:tc, :]
    if emit_bias_lanes:
        qa_ref, ka_ref = refs[:2]
        qa_ref[0], ka_ref[0] = _bias_lanes(f, d)
    else:
        fcol_ref, frow_ref = refs[:2]
        fcol_ref[0] = f
        frow_ref[0] = f.T[0:n_heads, :]


def _cumsum(logf, *, tc, d, emit_bias_lanes, n_heads=N_HEADS):
    b, s, w = logf.shape
    assert s % tc == 0 and tc % LANES == 0 and w == LANES
    if emit_bias_lanes:
        out_specs = [pl.BlockSpec((1, tc, d), lambda i, j: (i, j, 0))] * 2
        out_shape = [jax.ShapeDtypeStruct((b, s, d), BF16)] * 2
    else:
        out_specs = [pl.BlockSpec((1, tc, LANES), lambda i, j: (i, j, 0)),
                     pl.BlockSpec((1, n_heads, tc), lambda i, j: (i, 0, j))]
        out_shape = [jax.ShapeDtypeStruct((b, s, LANES), F32),
                     jax.ShapeDtypeStruct((b, n_heads, s), F32)]
    return pl.pallas_call(
        functools.partial(_cumsum_body, tc=tc, n_heads=n_heads, d=d,
                          emit_bias_lanes=emit_bias_lanes),
        grid=(b, s // tc),
        in_specs=[pl.BlockSpec((1, tc, LANES), lambda i, j: (i, j, 0))],
        out_specs=out_specs, out_shape=out_shape,
        scratch_shapes=[pltpu.VMEM((8, LANES), F32)],
        compiler_params=_params("parallel", "arbitrary"),
        name="logf_cumsum",
    )(logf)


def _fox_body(q_ref, qa_ref, k_ref, ka_ref, vt_ref, o_ref, qh_ref, acc_ref,
              s0_ref, s1_ref, m0_ref, m1_ref, a0_ref, a1_ref, *, tile):
    i = pl.program_id(2)
    bufs = ((s0_ref, m0_ref, a0_ref), (s1_ref, m1_ref, a1_ref))
    qw = tile // FOX_QUERY_SPLIT
    chains = [(hh, qs) for hh in range(HEADS_PER_BLOCK) for qs in range(FOX_QUERY_SPLIT)]
    lower = _lower_lanes((tile, LANES))
    q = q_ref[0]
    qa = qa_ref[0]
    qh_ref[0] = jnp.where(lower, q, qa)
    qh_ref[1] = jnp.where(lower, qa, q)
    acc_ref[...] = jnp.zeros_like(acc_ref)
    ones = jnp.ones((HEAD_DIM, tile), BF16)

    def score_step(j, dst, src, diagonal=False):
        s_ref, m_ref, a_ref = bufs[dst]
        off = pl.multiple_of(j * tile, tile)
        kb = k_ref[0, pl.ds(off, tile), :]
        ka = ka_ref[0, pl.ds(off, tile), :]
        kh = (jnp.where(lower, kb, ka), jnp.where(lower, ka, kb))
        for c, (hh, qs) in enumerate(chains):
            s = _nt_dot(kh[hh], qh_ref[hh, qs * qw:(qs + 1) * qw, :])
            if diagonal:
                key = lax.broadcasted_iota(jnp.int32, s.shape, 0)
                qry = lax.broadcasted_iota(jnp.int32, s.shape, 1) + qs * qw
                s = jnp.where(key <= qry, s, NEG_INF)
            m_prev = bufs[src][1][c]
            m_new = jnp.maximum(m_prev, jnp.max(s, axis=0, keepdims=True))
            s_ref[c] = s
            m_ref[c] = m_new
            a_ref[c] = jnp.exp(m_prev - m_new)

    def value_step(j, src):
        s_ref, m_ref, a_ref = bufs[src]
        off = pl.multiple_of(j * tile, tile)
        vt = vt_ref[0, :, pl.ds(off, tile)]
        vh = (jnp.concatenate([vt[:HEAD_DIM], ones], axis=0),
              jnp.concatenate([ones, vt[HEAD_DIM:]], axis=0))
        for c, (hh, qs) in enumerate(chains):
            pt = jnp.exp(s_ref[c] - m_ref[c]).astype(BF16)
            cols = slice(qs * qw, (qs + 1) * qw)
            acc_ref[hh, :, cols] = (a_ref[c] * acc_ref[hh, :, cols]
                                    + jnp.dot(vh[hh], pt, preferred_element_type=F32))

    block_of_step = lambda t: jnp.where(t == 0, i, t - 1)
    m1_ref[...] = jnp.full(m1_ref.shape, NEG_INF, F32)
    score_step(i, 0, 1, diagonal=True)

    def two_steps(pp, carry):
        t = 2 * pp
        score_step(t, 1, 0)
        value_step(block_of_step(t), 0)
        score_step(t + 1, 0, 1)
        value_step(t, 1)
        return carry

    lax.fori_loop(0, i // 2, two_steps, 0)

    @pl.when(i % 2 == 1)
    def _():
        score_step(i - 1, 1, 0)
        value_step(block_of_step(i - 1), 0)
        value_step(i - 1, 1)

    @pl.when(i % 2 == 0)
    def _():
        value_step(block_of_step(i), 0)

    o0 = acc_ref[0, :HEAD_DIM, :] / acc_ref[0, HEAD_DIM:, :]
    o1 = acc_ref[1, HEAD_DIM:, :] / acc_ref[1, :HEAD_DIM, :]
    o_ref[0] = jnp.concatenate([o0, o1], axis=0).T.astype(o_ref.dtype)


def _fox_attn_prompt(q, qa, k, ka, vt):
    b, s, d = q.shape
    tile = FOX_TILE
    assert s % tile == 0
    pairs = d // LANES
    n_chains = HEADS_PER_BLOCK * FOX_QUERY_SPLIT
    qw = tile // FOX_QUERY_SPLIT
    qblk = pl.BlockSpec((1, tile, LANES), lambda i, p, t: (i, t, p))
    seq = pl.BlockSpec((1, s, LANES), lambda i, p, t: (i, 0, p))
    return pl.pallas_call(
        functools.partial(_fox_body, tile=tile),
        grid=(b, pairs, s // tile),
        in_specs=[qblk, qblk, seq, seq, pl.BlockSpec((1, LANES, s), lambda i, p, t: (i, p, 0))],
        out_specs=qblk,
        out_shape=jax.ShapeDtypeStruct((b, s, d), BF16),
        scratch_shapes=[pltpu.VMEM((HEADS_PER_BLOCK, tile, LANES), BF16),
                        pltpu.VMEM((HEADS_PER_BLOCK, LANES, tile), F32)]
        + [pltpu.VMEM((n_chains, tile, qw), F32)] * 2
        + [pltpu.VMEM((n_chains, 1, qw), F32)] * 4,
        compiler_params=_params("parallel", "parallel", "parallel"),
        name="fox_attn",
    )(q, qa, k, ka, vt)


def _head_column(f_blk, head):
    lane = lax.broadcasted_iota(jnp.int32, f_blk.shape, 1)
    return jnp.sum(jnp.where(lane == head, f_blk, 0.0), axis=1, keepdims=True)


def _fox_sample_body(q_ref, kc_ref, vc_ref, kn_ref, vn_ref, fq_ref, fk_ref, o_ref, *, n_cache):
    p = pl.program_id(1)
    q = q_ref[0]
    t = q.shape[0]
    kc = kc_ref[0, 0].astype(BF16)
    vc = vc_ref[0, 0].astype(BF16)
    kn = kn_ref[0]
    vn = vn_ref[0]
    row = lax.broadcasted_iota(jnp.int32, (t, t), 0)
    col = lax.broadcasted_iota(jnp.int32, (t, t), 1)
    outs = []
    for hh in range(HEADS_PER_BLOCK):
        qm = jnp.where(_head_select(q.shape, hh), q, jnp.zeros_like(q))
        fq = _head_column(fq_ref[0], HEADS_PER_BLOCK * p + hh)
        sc = _nt_dot(qm, kc) + (fq - fk_ref[0, 0, hh:hh + 1, 0:n_cache])
        sn = _nt_dot(qm, kn) + (fq - fk_ref[0, 0, hh:hh + 1, n_cache:n_cache + t])
        sn = jnp.where(col <= row, sn, NEG_INF)
        m = jnp.maximum(jnp.max(sc, axis=-1, keepdims=True), jnp.max(sn, axis=-1, keepdims=True))
        pc = jnp.exp(sc - m)
        pn = jnp.exp(sn - m)
        l = jnp.sum(pc, axis=-1, keepdims=True) + jnp.sum(pn, axis=-1, keepdims=True)
        o = (jnp.dot(pc.astype(BF16), vc, preferred_element_type=F32)
             + jnp.dot(pn.astype(BF16), vn, preferred_element_type=F32))
        outs.append(o / l)
    o_ref[0] = jnp.where(_lower_lanes(outs[0].shape), outs[0], outs[1]).astype(o_ref.dtype)


def _fox_attn_sample(q, k, v, k_cache, v_cache, f_col_new, f_row, layer):
    b, t, d = q.shape
    n_cache = k_cache.shape[2]
    pairs = d // LANES
    width = f_row.shape[2]
    f_row = f_row.reshape(b, pairs, HEADS_PER_BLOCK, width)
    new = pl.BlockSpec((1, t, LANES), lambda i, p: (i, 0, p))
    cache = pl.BlockSpec((1, 1, n_cache, LANES), lambda i, p: (layer, i, 0, p))
    return pl.pallas_call(
        functools.partial(_fox_sample_body, n_cache=n_cache),
        grid=(b, pairs),
        in_specs=[new, cache, cache, new, new,
                  pl.BlockSpec((1, t, LANES), lambda i, p: (i, 0, 0)),
                  pl.BlockSpec((1, 1, HEADS_PER_BLOCK, width), lambda i, p: (i, p, 0, 0))],
        out_specs=new,
        out_shape=jax.ShapeDtypeStruct((b, t, d), BF16),
        compiler_params=_params("parallel", "parallel"),
        name="fox_attn_sample",
    )(q, k_cache, v_cache, k, v, f_col_new, f_row)


def _trunk(x, mods, weights, a_mix, b_mix, *, ts, keep, v_name):
    b, s, d = x.shape
    depth = mods.shape[0]
    heads = lambda a: a.reshape(a.shape[0], a.shape[1], N_HEADS, HEAD_DIM)
    a_state, b_state = [], []
    for l in range(depth):
        mod = mods[l]
        g, w = weights["g_mix"][l], weights["w_qkv"][l]
        if l % 2 == 0:
            if keep == s:
                proj = kv32 = _qkv_proj(x, mod, g, w, ts=ts, emit=("q", "k", v_name, "k32", "v32"))
            else:
                proj = _qkv_proj(x, mod, g, w, ts=ts, emit=("q", "k", v_name))
                kv32 = _qkv_proj(x, mod, g, w, ts=min(ts, keep), emit=("k32", "v32"),
                                 row_start=s - keep, rows=keep)
            a_state.append((heads(kv32["k32"]), heads(kv32["v32"])))
            o = a_mix(l // 2, proj)
        else:
            ib = l // 2
            proj = _qkv_proj(x, mod, g, w, weights["w_fgate"][ib], weights["b_fgate"][ib],
                             ts=ts, emit=("q", "k", v_name, "k32", "v32", "logf"))
            b_state.append((heads(proj["k32"]), heads(proj["v32"]), proj["logf"][:, :, :N_HEADS]))
            o = b_mix(ib, proj)
        x = _out_proj(o, x, mod, weights["w_o"][l], ts=ts)
        g_out = weights["g_out"] if l == depth - 1 else None
        x = _ffn(x, mod, weights["g_ffn"][l], weights["w_gu"][l], weights["w_down"][l], g_out, ts=ts)
    return x, a_state, b_state


def kernel(x_prompt, x_sample, cache_a_k, cache_a_v, cache_b_k, cache_b_v, cache_b_logf,
           c_prompt, c_sample, w_mod, b_mod, g_mix, g_ffn, w_qkv, w_o, rel_bias,
           w_fgate, b_fgate, w_gu, w_down, g_out):
    bp, sp, d = x_prompt.shape
    bs, t_new, _ = x_sample.shape
    depth = w_mod.shape[0]
    n_b = w_fgate.shape[0]
    assert d == N_HEADS * HEAD_DIM
    keep = min(A_PAST, sp)

    weights = {
        "g_mix": g_mix, "g_ffn": g_ffn, "g_out": g_out,
        "w_qkv": w_qkv.astype(BF16), "w_o": w_o.astype(BF16),
        "w_gu": w_gu.astype(BF16), "w_down": w_down.astype(BF16),
        "w_fgate": jnp.pad(w_fgate, ((0, 0), (0, 0), (0, LANES - N_HEADS))).astype(BF16),
        "b_fgate": jnp.pad(b_fgate, ((0, 0), (0, LANES - N_HEADS))).reshape(n_b, 1, LANES),
    }

    c_all = jnp.concatenate([c_prompt, c_sample], axis=0)
    rows = -(-c_all.shape[0] // 16) * 16
    c_all = jnp.pad(c_all, ((0, rows - c_all.shape[0]), (0, 0)))
    mods = _modulation(c_all, w_mod, b_mod)
    mods_p = mods[:, :bp].reshape(depth, bp, 6, d)
    mods_s = mods[:, bp:bp + bs].reshape(depth, bs, 6, d)

    bias = [_rel_bias_tile(rel_bias[i]) for i in range(rel_bias.shape[0])]

    def a_prompt(i, proj):
        return _chunk_attn_prompt(proj["q"], proj["k"], proj["vt"], bias[i][1])

    def b_prompt(i, proj):
        qa, ka = _cumsum(proj["logf"], tc=FOX_TILE, d=d, emit_bias_lanes=True)
        return _fox_attn_prompt(proj["q"], qa, proj["k"], ka, proj["vt"])

    y_prompt, pa, pb = _trunk(x_prompt, mods_p, weights, a_prompt, b_prompt, ts=ROW_TILE,
                              keep=keep, v_name="vt")

    n_a = cache_a_k.shape[0]
    ca_k = cache_a_k.reshape(n_a, bs, cache_a_k.shape[2], d)
    ca_v = cache_a_v.reshape(n_a, bs, cache_a_v.shape[2], d)
    past = cache_b_k.shape[2]
    cb_k = cache_b_k.reshape(n_b, bs, past, d)
    cb_v = cache_b_v.reshape(n_b, bs, past, d)
    total = -(-(past + t_new) // SAMPLE_CUMSUM_TILE) * SAMPLE_CUMSUM_TILE

    def a_sample(i, proj):
        return _chunk_attn_sample(proj["q"], proj["k"], proj["v"], ca_k, ca_v, bias[i][0], i)

    def b_sample(i, proj):
        lf_cache = jnp.pad(cache_b_logf[i].astype(F32), ((0, 0), (0, 0), (0, LANES - N_HEADS)))
        lf = jnp.concatenate([lf_cache, proj["logf"]], axis=1)
        lf = jnp.pad(lf, ((0, 0), (0, total - (past + t_new)), (0, 0)))
        f_col, f_row = _cumsum(lf, tc=SAMPLE_CUMSUM_TILE, d=d, emit_bias_lanes=False)
        return _fox_attn_sample(proj["q"], proj["k"], proj["v"], cb_k, cb_v,
                                f_col[:, past:past + t_new], f_row, i)

    y_sample, sa, sb = _trunk(x_sample, mods_s, weights, a_sample, b_sample, ts=t_new,
                              keep=t_new, v_name="v")

    stack = lambda states, j: jnp.stack([st[j] for st in states])
    return (y_prompt, y_sample, stack(pa, 0), stack(pa, 1), stack(pb, 0), stack(pb, 1), stack(pb, 2),
            stack(sa, 0), stack(sa, 1), stack(sb, 0), stack(sb, 1), stack(sb, 2))
```

```python
import functools

import jax
import jax.numpy as jnp
from jax import lax
from jax.experimental import pallas as pl
from jax.experimental.pallas import tpu as pltpu

F32 = jnp.float32
BF16 = jnp.bfloat16

N_HEADS = 16
HEAD_DIM = 64
CHUNK = 64
N_PAST_CHUNKS = 8
A_PAST = N_PAST_CHUNKS * CHUNK
MAX_REL = 256
RMS_EPS = 1e-6
NEG_INF = -1e30
ATTN_SCALE = HEAD_DIM ** -0.5

LANES = 128
HEADS_PER_BLOCK = LANES // HEAD_DIM
VMEM_LIMIT_BYTES = 48 * 1024 * 1024

ROW_TILE = 512
A_TILE = 256
A_WIN = A_TILE + A_PAST
FOX_TILE = 512
FOX_QUERY_SPLIT = 2
FF_CHUNK = 256
REL_PAD = 384
G_WIDTH = 1024
SAMPLE_CUMSUM_TILE = 384


def _params(*sem):
    return pltpu.CompilerParams(dimension_semantics=sem, vmem_limit_bytes=VMEM_LIMIT_BYTES)


def _resident(shape):
    zeros = (0,) * len(shape)
    return pl.BlockSpec(shape, lambda *_: zeros, pipeline_mode=pl.Buffered(1))


def _nt_dot(a, b):
    return lax.dot_general(a, b, (((1,), (1,)), ((), ())), preferred_element_type=F32)


def _modulated_norm(x, g, scale, shift):
    ms = jnp.mean(x * x, axis=-1, keepdims=True)
    y = x * lax.rsqrt(ms + RMS_EPS) * g
    return y * (1.0 + scale) + shift


def _split3(x):
    hi = x.astype(BF16)
    r1 = x - hi.astype(F32)
    mid = r1.astype(BF16)
    lo = (r1 - mid.astype(F32)).astype(BF16)
    return hi, mid, lo


def _lower_lanes(shape):
    return lax.broadcasted_iota(jnp.int32, shape, len(shape) - 1) < HEAD_DIM


def _mod_body(c_ref, w_ref, b_ref, o_ref):
    cs = jax.nn.silu(c_ref[...]).astype(BF16)
    o_ref[0] = jnp.dot(cs, w_ref[0].astype(BF16), preferred_element_type=F32) + b_ref[0]


def _modulation(c, w_mod, b_mod):
    depth, d, n = w_mod.shape
    rows = c.shape[0]
    tn = 1536
    assert n % tn == 0 and rows % 16 == 0
    return pl.pallas_call(
        _mod_body,
        grid=(depth, n // tn),
        in_specs=[pl.BlockSpec((rows, d), lambda l, j: (0, 0)),
                  pl.BlockSpec((1, d, tn), lambda l, j: (l, 0, j)),
                  pl.BlockSpec((1, 1, tn), lambda l, j: (l, 0, j))],
        out_specs=pl.BlockSpec((1, rows, tn), lambda l, j: (l, 0, j)),
        out_shape=jax.ShapeDtypeStruct((depth, rows, n), F32),
        compiler_params=_params("parallel", "parallel"),
        name="modulation",
    )(c, w_mod, b_mod.reshape(depth, 1, n))


def _qkv_body(*refs, d, emit):
    x_ref, mod_ref, g_ref, w_ref = refs[:4]
    pos = 4
    if "logf" in emit:
        wf_ref, bf_ref = refs[4:6]
        pos = 6
    out = dict(zip(emit, refs[pos:]))
    x = x_ref[...]
    bb, ts, _ = x.shape
    h = _modulated_norm(x, g_ref[...], mod_ref[:, 1:2, :], mod_ref[:, 0:1, :])
    hb = h.reshape(bb * ts, d).astype(BF16)

    def put(name, val, dtype):
        out[name][...] = val.reshape(bb, ts, val.shape[-1]).astype(dtype)

    if "q" in emit:
        q = jnp.dot(hb, w_ref[:, 0:d], preferred_element_type=F32)
        put("q", q * ATTN_SCALE, BF16)
    k = jnp.dot(hb, w_ref[:, d:2 * d], preferred_element_type=F32)
    v = jnp.dot(hb, w_ref[:, 2 * d:3 * d], preferred_element_type=F32)
    if "k" in emit:
        put("k", k, BF16)
    if "v" in emit:
        put("v", v, BF16)
    if "vt" in emit:
        out["vt"][0] = v.T.astype(BF16)
    if "k32" in emit:
        put("k32", k, F32)
    if "v32" in emit:
        put("v32", v, F32)
    if "logf" in emit:
        z = jnp.dot(hb, wf_ref[...], preferred_element_type=F32) + bf_ref[...]
        put("logf", jax.nn.log_sigmoid(z), F32)


def _qkv_proj(x, mod, g, w_qkv, wf=None, bf=None, *, bb, ts, emit, row_start=0, rows=None):
    b, s, d = x.shape
    rows = s if rows is None else rows
    assert rows % ts == 0 and row_start % ts == 0 and b % bb == 0
    assert ts % 16 == 0 or bb == 1
    assert "vt" not in emit or bb == 1
    off = row_start // ts
    assert ("logf" in emit) == (wf is not None)
    row_spec = lambda width: pl.BlockSpec((bb, ts, width), lambda i, j: (i, j, 0))
    in_specs = [pl.BlockSpec((bb, ts, d), lambda i, j: (i, j + off, 0)),
                pl.BlockSpec((bb, 6, d), lambda i, j: (i, 0, 0)),
                _resident((1, d)),
                _resident((d, 3 * d))]
    args = [x, mod, g.reshape(1, d), w_qkv]
    if wf is not None:
        in_specs += [_resident((d, LANES)), _resident((1, LANES))]
        args += [wf, bf]
    kinds = {
        "q": (row_spec(d), (b, rows, d), BF16), "k": (row_spec(d), (b, rows, d), BF16),
        "v": (row_spec(d), (b, rows, d), BF16),
        "vt": (pl.BlockSpec((1, d, ts), lambda i, j: (i, 0, j)), (b, d, rows), BF16),
        "k32": (row_spec(d), (b, rows, d), F32), "v32": (row_spec(d), (b, rows, d), F32),
        "logf": (row_spec(LANES), (b, rows, LANES), F32),
    }
    outs = pl.pallas_call(
        functools.partial(_qkv_body, d=d, emit=emit),
        grid=(b // bb, rows // ts),
        in_specs=in_specs,
        out_specs=[kinds[e][0] for e in emit],
        out_shape=[jax.ShapeDtypeStruct(kinds[e][1], kinds[e][2]) for e in emit],
        compiler_params=_params("parallel", "parallel"),
        name="norm_qkv",
    )(*args)
    return dict(zip(emit, outs))


def _post_attn_body(*refs, d_ff, final_norm):
    o_ref, x_ref, mod_ref, wo_ref, g_ref, wgu_ref, wd_ref = refs[:7]
    out_ref = refs[-1]
    x = x_ref[...]
    bb, ts, d = x.shape
    mod = lambda r: mod_ref[:, r:r + 1, :]
    y = jnp.dot(o_ref[...].reshape(bb * ts, d), wo_ref[...], preferred_element_type=F32)
    x1 = x + mod(2) * y.reshape(bb, ts, d)
    h = _modulated_norm(x1, g_ref[...], mod(4), mod(3))
    hb = h.reshape(bb * ts, d).astype(BF16)
    acc = jnp.zeros((bb * ts, d), F32)
    for c in range(d_ff // FF_CHUNK):
        lo = c * FF_CHUNK
        gate = jnp.dot(hb, wgu_ref[:, lo:lo + FF_CHUNK], preferred_element_type=F32)
        up = jnp.dot(hb, wgu_ref[:, d_ff + lo:d_ff + lo + FF_CHUNK], preferred_element_type=F32)
        act = (jax.nn.silu(gate) * up).astype(BF16)
        acc = acc + jnp.dot(act, wd_ref[lo:lo + FF_CHUNK, :], preferred_element_type=F32)
    x2 = x1 + mod(5) * acc.reshape(bb, ts, d)
    if final_norm:
        gout_ref = refs[7]
        ms = jnp.mean(x2 * x2, axis=-1, keepdims=True)
        x2 = x2 * lax.rsqrt(ms + RMS_EPS) * gout_ref[...]
    out_ref[...] = x2


def _post_attn(o, x, mod, w_o, g, w_gu, w_down, g_out=None, *, bb, ts):
    b, s, d = x.shape
    d_ff = w_down.shape[0]
    assert s % ts == 0 and b % bb == 0 and d_ff % FF_CHUNK == 0
    assert ts % 16 == 0 or bb == 1
    row = pl.BlockSpec((bb, ts, d), lambda i, j: (i, j, 0))
    in_specs = [row, row, pl.BlockSpec((bb, 6, d), lambda i, j: (i, 0, 0)), _resident((d, d)),
                _resident((1, d)), _resident((d, 2 * d_ff)), _resident((d_ff, d))]
    args = [o, x, mod, w_o, g.reshape(1, d), w_gu, w_down]
    if g_out is not None:
        in_specs.append(_resident((1, d)))
        args.append(g_out.reshape(1, d))
    return pl.pallas_call(
        functools.partial(_post_attn_body, d_ff=d_ff, final_norm=g_out is not None),
        grid=(b // bb, s // ts),
        in_specs=in_specs, out_specs=row,
        out_shape=jax.ShapeDtypeStruct((b, s, d), F32),
        compiler_params=_params("parallel", "parallel"),
        name="oproj_ffn",
    )(*args)


def _bias_row_body(tab_ref, g_ref):
    u = lax.broadcasted_iota(jnp.int32, (REL_PAD, G_WIDTH), 1)
    r = lax.broadcasted_iota(jnp.int32, (REL_PAD, G_WIDTH), 0)
    rel = A_PAST + (A_TILE - 1) - u
    idx = jnp.clip(rel, -(CHUNK - 1), MAX_REL) + (CHUNK - 1)
    onehot = jnp.where(r == idx, 1.0, 0.0).astype(BF16)
    hi, mid, lo = _split3(tab_ref[...])
    g_ref[...] = (jnp.dot(hi, onehot, preferred_element_type=F32)
                  + jnp.dot(mid, onehot, preferred_element_type=F32)
                  + jnp.dot(lo, onehot, preferred_element_type=F32))


def _bias_tile_body(g_ref, o_ref, ot_ref):
    x = jnp.broadcast_to(g_ref[0], (A_TILE, G_WIDTH))
    rolled = pltpu.roll(x, G_WIDTH - (A_TILE - 1), 1, stride=1, stride_axis=0)
    val = rolled[:, :A_WIN]
    qc = lax.broadcasted_iota(jnp.int32, (A_TILE, A_WIN), 0) // CHUNK
    kc = lax.broadcasted_iota(jnp.int32, (A_TILE, A_WIN), 1) // CHUNK
    band = (kc >= qc) & (kc <= qc + N_PAST_CHUNKS)
    tile = jnp.where(band, val, NEG_INF)
    o_ref[0] = tile
    ot_ref[0] = tile.T


def _rel_bias_tile(table):
    h, rel = table.shape
    assert rel == CHUNK + MAX_REL and rel <= REL_PAD and A_WIN + A_TILE - 1 <= G_WIDTH
    tab = jnp.pad(table, ((0, 0), (0, REL_PAD - rel)))
    gen = pl.pallas_call(
        _bias_row_body,
        out_shape=jax.ShapeDtypeStruct((h, G_WIDTH), F32),
        name="rel_bias_row",
    )(tab)
    return pl.pallas_call(
        _bias_tile_body,
        grid=(h,),
        in_specs=[pl.BlockSpec((1, 1, G_WIDTH), lambda i: (i, 0, 0))],
        out_specs=[pl.BlockSpec((1, A_TILE, A_WIN), lambda i: (i, 0, 0)),
                   pl.BlockSpec((1, A_WIN, A_TILE), lambda i: (i, 0, 0))],
        out_shape=[jax.ShapeDtypeStruct((h, A_TILE, A_WIN), F32),
                   jax.ShapeDtypeStruct((h, A_WIN, A_TILE), F32)],
        compiler_params=_params("parallel"),
        name="rel_bias_tile",
    )(gen.reshape(h, 1, G_WIDTH))


def _head_select(shape, hh):
    lower = _lower_lanes(shape)
    return lower if hh == 0 else jnp.logical_not(lower)


def _chunk_attn_body(q_ref, k_ref, vt_ref, bias_ref, o_ref, s0_ref, s1_ref, m0_ref, m1_ref,
                     *, n_tiles):
    sets = ((s0_ref, m0_ref), (s1_ref, m1_ref))
    lower = _lower_lanes((A_TILE, LANES))

    def head_queries(q):
        zero = jnp.zeros_like(q)
        return jnp.where(lower, q, zero), jnp.where(lower, zero, q)

    def head_values(vt):
        ones = jnp.ones((HEAD_DIM, vt.shape[1]), BF16)
        return (jnp.concatenate([vt[:HEAD_DIM], ones], axis=0),
                jnp.concatenate([ones, vt[HEAD_DIM:]], axis=0))

    def finish(accs):
        o0 = accs[0][:HEAD_DIM] / accs[0][HEAD_DIM:]
        o1 = accs[1][HEAD_DIM:] / accs[1][:HEAD_DIM]
        return jnp.concatenate([o0, o1], axis=0).T.astype(o_ref.dtype)

    def edge_tile(t):
        nk = (t + 1) * A_TILE
        qm = head_queries(q_ref[0, t * A_TILE:(t + 1) * A_TILE, :])
        kw = k_ref[0, 0:nk, :]
        vh = head_values(vt_ref[0, :, 0:nk])
        accs = []
        for hh in range(HEADS_PER_BLOCK):
            s = _nt_dot(kw, qm[hh]) + bias_ref[hh, A_WIN - nk:A_WIN, :]
            pt = jnp.exp(s - jnp.max(s, axis=0, keepdims=True)).astype(BF16)
            accs.append(jnp.dot(vh[hh], pt, preferred_element_type=F32))
        o_ref[0, t * A_TILE:(t + 1) * A_TILE, :] = finish(accs)

    def score_step(t, dst):
        s_ref, m_ref = sets[dst]
        qm = head_queries(q_ref[0, pl.ds(pl.multiple_of(t * A_TILE, A_TILE), A_TILE), :])
        kw = k_ref[0, pl.ds(pl.multiple_of((t - 2) * A_TILE, A_TILE), A_WIN), :]
        for hh in range(HEADS_PER_BLOCK):
            s = _nt_dot(kw, qm[hh]) + bias_ref[hh]
            s_ref[hh] = s
            m_ref[hh] = jnp.max(s, axis=0, keepdims=True)

    def value_step(t, src):
        s_ref, m_ref = sets[src]
        vh = head_values(vt_ref[0, :, pl.ds(pl.multiple_of((t - 2) * A_TILE, A_TILE), A_WIN)])
        accs = []
        for hh in range(HEADS_PER_BLOCK):
            pt = jnp.exp(s_ref[hh] - m_ref[hh]).astype(BF16)
            accs.append(jnp.dot(vh[hh], pt, preferred_element_type=F32))
        o_ref[0, pl.ds(pl.multiple_of(t * A_TILE, A_TILE), A_TILE), :] = finish(accs)

    edge_tile(0)
    edge_tile(1)
    score_step(2, 0)

    def two_tiles(u, carry):
        t = 2 + 2 * u
        score_step(t + 1, 1)
        value_step(t, 0)
        score_step(t + 2, 0)
        value_step(t + 1, 1)
        return carry

    lax.fori_loop(0, (n_tiles - 4) // 2, two_tiles, 0)
    score_step(n_tiles - 1, 1)
    value_step(n_tiles - 2, 0)
    value_step(n_tiles - 1, 1)


def _chunk_attn_prompt(q, k, vt, bias_t):
    b, s, d = q.shape
    n_tiles = s // A_TILE
    assert A_WIN == 3 * A_TILE and s % A_TILE == 0 and d % LANES == 0
    assert n_tiles >= 4 and n_tiles % 2 == 0
    pairs = d // LANES
    rows = pl.BlockSpec((1, s, LANES), lambda p, i: (i, 0, p))
    return pl.pallas_call(
        functools.partial(_chunk_attn_body, n_tiles=n_tiles),
        grid=(pairs, b),
        in_specs=[rows, rows, pl.BlockSpec((1, LANES, s), lambda p, i: (i, p, 0)),
                  pl.BlockSpec((HEADS_PER_BLOCK, A_WIN, A_TILE), lambda p, i: (p, 0, 0))],
        out_specs=rows,
        out_shape=jax.ShapeDtypeStruct((b, s, d), BF16),
        scratch_shapes=[pltpu.VMEM((HEADS_PER_BLOCK, A_WIN, A_TILE), F32)] * 2
        + [pltpu.VMEM((HEADS_PER_BLOCK, 1, A_TILE), F32)] * 2,
        compiler_params=_params("parallel", "parallel"),
        name="chunk_attn",
    )(q, k, vt, bias_t)


def _sample_attn_body(*refs, n_cache, forget):
    q_ref, kc_ref, vc_ref, kn_ref, vn_ref = refs[:5]
    o_ref = refs[-1]
    t, d = q_ref.shape[1:]
    if forget:
        fq_ref, fk_ref = refs[5:7]
        causal = (lax.broadcasted_iota(jnp.int32, (t, t), 1)
                  <= lax.broadcasted_iota(jnp.int32, (t, t), 0))
    else:
        bias_ref = refs[5]
    for p in range(d // LANES):
        lanes = slice(p * LANES, (p + 1) * LANES)
        q = q_ref[0, :, lanes]
        kc = kc_ref[0, 0, :, lanes].astype(BF16)
        vc = vc_ref[0, 0, :, lanes].astype(BF16)
        kn = kn_ref[0, :, lanes]
        vn = vn_ref[0, :, lanes]
        outs = []
        for hh in range(HEADS_PER_BLOCK):
            head = HEADS_PER_BLOCK * p + hh
            qm = jnp.where(_head_select(q.shape, hh), q, jnp.zeros_like(q))
            if forget:
                fq = fq_ref[0, :, head:head + 1]
                sc = _nt_dot(qm, kc) + (fq - fk_ref[0, head:head + 1, 0:n_cache])
                sn = _nt_dot(qm, kn) + (fq - fk_ref[0, head:head + 1, n_cache:n_cache + t])
                sn = jnp.where(causal, sn, NEG_INF)
            else:
                sc = _nt_dot(qm, kc) + bias_ref[head, :, 0:n_cache]
                sn = _nt_dot(qm, kn) + bias_ref[head, :, n_cache:n_cache + t]
            m = jnp.maximum(jnp.max(sc, axis=-1, keepdims=True),
                            jnp.max(sn, axis=-1, keepdims=True))
            pc = jnp.exp(sc - m)
            pn = jnp.exp(sn - m)
            l = jnp.sum(pc, axis=-1, keepdims=True) + jnp.sum(pn, axis=-1, keepdims=True)
            o = (jnp.dot(pc.astype(BF16), vc, preferred_element_type=F32)
                 + jnp.dot(pn.astype(BF16), vn, preferred_element_type=F32))
            outs.append(o / l)
        o_ref[0, :, lanes] = jnp.where(_lower_lanes(outs[0].shape), outs[0],
                                       outs[1]).astype(o_ref.dtype)


def _sample_attn(q, k, v, k_cache, v_cache, layer, *, bias=None, f_col_new=None, f_row=None):
    b, t, d = q.shape
    n_cache = k_cache.shape[2]
    new = pl.BlockSpec((1, t, d), lambda i: (i, 0, 0))
    cache = pl.BlockSpec((1, 1, n_cache, d), lambda i: (layer, i, 0, 0))
    forget = bias is None
    if forget:
        extra_specs = [pl.BlockSpec((1, t, LANES), lambda i: (i, 0, 0)),
                       pl.BlockSpec((1,) + f_row.shape[1:], lambda i: (i, 0, 0))]
        extra = [f_col_new, f_row]
    else:
        assert n_cache == A_PAST and t <= CHUNK and n_cache + t <= A_WIN
        extra_specs = [pl.BlockSpec((bias.shape[0], t, A_WIN), lambda i: (0, 0, 0))]
        extra = [bias]
    return pl.pallas_call(
        functools.partial(_sample_attn_body, n_cache=n_cache, forget=forget),
        grid=(b,),
        in_specs=[new, cache, cache, new, new] + extra_specs,
        out_specs=new,
        out_shape=jax.ShapeDtypeStruct((b, t, d), BF16),
        compiler_params=_params("parallel"),
        name="sample_attn",
    )(q, k_cache, v_cache, k, v, *extra)


def _bias_lanes(f, d):
    hi, mid, lo = _split3(f)
    col = lax.broadcasted_iota(jnp.int32, (LANES, d), 1)
    row = lax.broadcasted_iota(jnp.int32, (LANES, d), 0)
    lane = col % LANES
    head = HEADS_PER_BLOCK * (col // LANES) + jnp.where(lane >= HEAD_DIM, 0, 1)
    slot = lane % HEAD_DIM
    hit = row == head

    def pick(first):
        out = None
        for n, piece in enumerate((hi, mid, lo)):
            sel = jnp.where(hit & (slot == first + n), 1.0, 0.0).astype(BF16)
            term = jnp.dot(piece, sel, preferred_element_type=F32)
            out = term if out is None else out + term
        return out

    slot_row = lax.broadcasted_iota(jnp.int32, (1, d), 1) % HEAD_DIM
    ones = lambda first: jnp.where((slot_row >= first) & (slot_row < first + 3), 1.0, 0.0)
    q_side = pick(0) + ones(3)
    k_side = ones(0) - pick(3)
    return q_side.astype(BF16), k_side.astype(BF16)


def _cumsum_body(x_ref, *refs, tc, n_heads, d, emit_bias_lanes):
    carry_ref = refs[-1]

    @pl.when(pl.program_id(1) == 0)
    def _():
        carry_ref[...] = jnp.zeros_like(carry_ref)

    r = lax.broadcasted_iota(jnp.int32, (tc, tc), 0)
    c = lax.broadcasted_iota(jnp.int32, (tc, tc), 1)
    tri = jnp.where(c <= r, 1.0, 0.0).astype(BF16)
    hi, mid, lo = _split3(x_ref[0])
    f = (jnp.dot(tri, hi, preferred_element_type=F32)
         + jnp.dot(tri, mid, preferred_element_type=F32)
         + jnp.dot(tri, lo, preferred_element_type=F32)) + carry_ref[0:1, :]
    carry_ref[0:1, :] = f[tc - 1:tc, :]
    if emit_bias_lanes:
        qa_ref, ka_ref = refs[:2]
        qa_ref[0], ka_ref[0] = _bias_lanes(f, d)
    else:
        fcol_ref, frow_ref = refs[:2]
        fcol_ref[0] = f
        frow_ref[0] = f.T[0:n_heads, :]


def _cumsum(logf, *, tc, d, emit_bias_lanes, n_heads=N_HEADS):
    b, s, w = logf.shape
    assert s % tc == 0 and tc % LANES == 0 and w == LANES
    if emit_bias_lanes:
        out_specs = [pl.BlockSpec((1, tc, d), lambda i, j: (i, j, 0))] * 2
        out_shape = [jax.ShapeDtypeStruct((b, s, d), BF16)] * 2
    else:
        out_specs = [pl.BlockSpec((1, tc, LANES), lambda i, j: (i, j, 0)),
                     pl.BlockSpec((1, n_heads, tc), lambda i, j: (i, 0, j))]
        out_shape = [jax.ShapeDtypeStruct((b, s, LANES), F32),
                     jax.ShapeDtypeStruct((b, n_heads, s), F32)]
    return pl.pallas_call(
        functools.partial(_cumsum_body, tc=tc, n_heads=n_heads, d=d,
                          emit_bias_lanes=emit_bias_lanes),
        grid=(b, s // tc),
        in_specs=[pl.BlockSpec((1, tc, LANES), lambda i, j: (i, j, 0))],
        out_specs=out_specs, out_shape=out_shape,
        scratch_shapes=[pltpu.VMEM((8, LANES), F32)],
        compiler_params=_params("parallel", "arbitrary"),
        name="logf_cumsum",
    )(logf)


def _fox_body(q_ref, qa_ref, k_ref, ka_ref, vt_ref, o_ref, qh_ref, acc_ref,
              s0_ref, s1_ref, m0_ref, m1_ref, a0_ref, a1_ref, *, tile):
    i = pl.program_id(2)
    bufs = ((s0_ref, m0_ref, a0_ref), (s1_ref, m1_ref, a1_ref))
    qw = tile // FOX_QUERY_SPLIT
    chains = [(hh, qs) for hh in range(HEADS_PER_BLOCK) for qs in range(FOX_QUERY_SPLIT)]
    lower = _lower_lanes((tile, LANES))
    q = q_ref[0]
    qa = qa_ref[0]
    qh_ref[0] = jnp.where(lower, q, qa)
    qh_ref[1] = jnp.where(lower, qa, q)
    acc_ref[...] = jnp.zeros_like(acc_ref)
    ones = jnp.ones((HEAD_DIM, tile), BF16)

    def score_step(j, dst, src, diagonal=False):
        s_ref, m_ref, a_ref = bufs[dst]
        off = pl.multiple_of(j * tile, tile)
        kb = k_ref[0, pl.ds(off, tile), :]
        ka = ka_ref[0, pl.ds(off, tile), :]
        kh = (jnp.where(lower, kb, ka), jnp.where(lower, ka, kb))
        for c, (hh, qs) in enumerate(chains):
            s = _nt_dot(kh[hh], qh_ref[hh, qs * qw:(qs + 1) * qw, :])
            if diagonal:
                key = lax.broadcasted_iota(jnp.int32, s.shape, 0)
                qry = lax.broadcasted_iota(jnp.int32, s.shape, 1) + qs * qw
                s = jnp.where(key <= qry, s, NEG_INF)
            m_prev = bufs[src][1][c]
            m_new = jnp.maximum(m_prev, jnp.max(s, axis=0, keepdims=True))
            s_ref[c] = s
            m_ref[c] = m_new
            a_ref[c] = jnp.exp(m_prev - m_new)

    def value_step(j, src):
        s_ref, m_ref, a_ref = bufs[src]
        off = pl.multiple_of(j * tile, tile)
        vt = vt_ref[0, :, pl.ds(off, tile)]
        vh = (jnp.concatenate([vt[:HEAD_DIM], ones], axis=0),
              jnp.concatenate([ones, vt[HEAD_DIM:]], axis=0))
        for c, (hh, qs) in enumerate(chains):
            pt = jnp.exp(s_ref[c] - m_ref[c]).astype(BF16)
            cols = slice(qs * qw, (qs + 1) * qw)
            acc_ref[hh, :, cols] = (a_ref[c] * acc_ref[hh, :, cols]
                                    + jnp.dot(vh[hh], pt, preferred_element_type=F32))

    block_of_step = lambda t: jnp.where(t == 0, i, t - 1)
    m1_ref[...] = jnp.full(m1_ref.shape, NEG_INF, F32)
    score_step(i, 0, 1, diagonal=True)

    def two_steps(pp, carry):
        t = 2 * pp
        score_step(t, 1, 0)
        value_step(block_of_step(t), 0)
        score_step(t + 1, 0, 1)
        value_step(t, 1)
        return carry

    lax.fori_loop(0, i // 2, two_steps, 0)

    @pl.when(i % 2 == 1)
    def _():
        score_step(i - 1, 1, 0)
        value_step(block_of_step(i - 1), 0)
        value_step(i - 1, 1)

    @pl.when(i % 2 == 0)
    def _():
        value_step(block_of_step(i), 0)

    o0 = acc_ref[0, :HEAD_DIM, :] / acc_ref[0, HEAD_DIM:, :]
    o1 = acc_ref[1, HEAD_DIM:, :] / acc_ref[1, :HEAD_DIM, :]
    o_ref[0] = jnp.concatenate([o0, o1], axis=0).T.astype(o_ref.dtype)


def _fox_attn_prompt(q, qa, k, ka, vt):
    b, s, d = q.shape
    tile = FOX_TILE
    assert s % tile == 0
    pairs = d // LANES
    n_chains = HEADS_PER_BLOCK * FOX_QUERY_SPLIT
    qw = tile // FOX_QUERY_SPLIT
    qblk = pl.BlockSpec((1, tile, LANES), lambda i, p, t: (i, t, p))
    seq = pl.BlockSpec((1, s, LANES), lambda i, p, t: (i, 0, p))
    return pl.pallas_call(
        functools.partial(_fox_body, tile=tile),
        grid=(b, pairs, s // tile),
        in_specs=[qblk, qblk, seq, seq, pl.BlockSpec((1, LANES, s), lambda i, p, t: (i, p, 0))],
        out_specs=qblk,
        out_shape=jax.ShapeDtypeStruct((b, s, d), BF16),
        scratch_shapes=[pltpu.VMEM((HEADS_PER_BLOCK, tile, LANES), BF16),
                        pltpu.VMEM((HEADS_PER_BLOCK, LANES, tile), F32)]
        + [pltpu.VMEM((n_chains, tile, qw), F32)] * 2
        + [pltpu.VMEM((n_chains, 1, qw), F32)] * 4,
        compiler_params=_params("parallel", "parallel", "parallel"),
        name="fox_attn",
    )(q, qa, k, ka, vt)


def _trunk(x, mods, weights, a_mix, b_mix, *, bb, ts, keep, v_name):
    b, s, d = x.shape
    depth = mods.shape[0]
    heads = lambda a: a.reshape(a.shape[0], a.shape[1], N_HEADS, HEAD_DIM)
    a_state, b_state = [], []
    for l in range(depth):
        mod = mods[l]
        g, w = weights["g_mix"][l], weights["w_qkv"][l]
        if l % 2 == 0:
            if keep == s:
                proj = kv32 = _qkv_proj(x, mod, g, w, bb=bb, ts=ts,
                                        emit=("q", "k", v_name, "k32", "v32"))
            else:
                proj = _qkv_proj(x, mod, g, w, bb=bb, ts=ts, emit=("q", "k", v_name))
                kv32 = _qkv_proj(x, mod, g, w, bb=bb, ts=min(ts, keep), emit=("k32", "v32"),
                                 row_start=s - keep, rows=keep)
            a_state.append((heads(kv32["k32"]), heads(kv32["v32"])))
            o = a_mix(l // 2, proj)
        else:
            ib = l // 2
            proj = _qkv_proj(x, mod, g, w, weights["w_fgate"][ib], weights["b_fgate"][ib],
                             bb=bb, ts=ts, emit=("q", "k", v_name, "k32", "v32", "logf"))
            b_state.append((heads(proj["k32"]), heads(proj["v32"]), proj["logf"][:, :, :N_HEADS]))
            o = b_mix(ib, proj)
        g_out = weights["g_out"] if l == depth - 1 else None
        x = _post_attn(o, x, mod, weights["w_o"][l], weights["g_ffn"][l], weights["w_gu"][l],
                       weights["w_down"][l], g_out, bb=bb, ts=ts)
    return x, a_state, b_state


def kernel(x_prompt, x_sample, cache_a_k, cache_a_v, cache_b_k, cache_b_v, cache_b_logf,
           c_prompt, c_sample, w_mod, b_mod, g_mix, g_ffn, w_qkv, w_o, rel_bias,
           w_fgate, b_fgate, w_gu, w_down, g_out):
    bp, sp, d = x_prompt.shape
    bs, t_new, _ = x_sample.shape
    depth = w_mod.shape[0]
    n_b = w_fgate.shape[0]
    assert d == N_HEADS * HEAD_DIM
    keep = min(A_PAST, sp)

    weights = {
        "g_mix": g_mix, "g_ffn": g_ffn, "g_out": g_out,
        "w_qkv": w_qkv.astype(BF16), "w_o": w_o.astype(BF16),
        "w_gu": w_gu.astype(BF16), "w_down": w_down.astype(BF16),
        "w_fgate": jnp.pad(w_fgate, ((0, 0), (0, 0), (0, LANES - N_HEADS))).astype(BF16),
        "b_fgate": jnp.pad(b_fgate, ((0, 0), (0, LANES - N_HEADS))).reshape(n_b, 1, LANES),
    }

    c_all = jnp.concatenate([c_prompt, c_sample], axis=0)
    rows = -(-c_all.shape[0] // 16) * 16
    c_all = jnp.pad(c_all, ((0, rows - c_all.shape[0]), (0, 0)))
    mods = _modulation(c_all, w_mod, b_mod)
    mods_p = mods[:, :bp].reshape(depth, bp, 6, d)
    mods_s = mods[:, bp:bp + bs].reshape(depth, bs, 6, d)

    bias = [_rel_bias_tile(rel_bias[i]) for i in range(rel_bias.shape[0])]

    def a_prompt(i, proj):
        return _chunk_attn_prompt(proj["q"], proj["k"], proj["vt"], bias[i][1])

    def b_prompt(i, proj):
        qa, ka = _cumsum(proj["logf"], tc=FOX_TILE, d=d, emit_bias_lanes=True)
        return _fox_attn_prompt(proj["q"], qa, proj["k"], ka, proj["vt"])

    y_prompt, pa, pb = _trunk(x_prompt, mods_p, weights, a_prompt, b_prompt, bb=1, ts=ROW_TILE,
                              keep=keep, v_name="vt")

    n_a = cache_a_k.shape[0]
    ca_k = cache_a_k.reshape(n_a, bs, cache_a_k.shape[2], d)
    ca_v = cache_a_v.reshape(n_a, bs, cache_a_v.shape[2], d)
    past = cache_b_k.shape[2]
    cb_k = cache_b_k.reshape(n_b, bs, past, d)
    cb_v = cache_b_v.reshape(n_b, bs, past, d)
    total = -(-(past + t_new) // SAMPLE_CUMSUM_TILE) * SAMPLE_CUMSUM_TILE

    def a_sample(i, proj):
        return _sample_attn(proj["q"], proj["k"], proj["v"], ca_k, ca_v, i, bias=bias[i][0])

    def b_sample(i, proj):
        lf_cache = jnp.pad(cache_b_logf[i].astype(F32), ((0, 0), (0, 0), (0, LANES - N_HEADS)))
        lf = jnp.concatenate([lf_cache, proj["logf"]], axis=1)
        lf = jnp.pad(lf, ((0, 0), (0, total - (past + t_new)), (0, 0)))
        f_col, f_row = _cumsum(lf, tc=SAMPLE_CUMSUM_TILE, d=d, emit_bias_lanes=False)
        return _sample_attn(proj["q"], proj["k"], proj["v"], cb_k, cb_v, i,
                            f_col_new=f_col[:, past:past + t_new], f_row=f_row)

    y_sample, sa, sb = _trunk(x_sample, mods_s, weights, a_sample, b_sample, bb=bs, ts=t_new,
                              keep=t_new, v_name="v")

    stack = lambda states, j: jnp.stack([st[j] for st in states])
    return (y_prompt, y_sample, stack(pa, 0), stack(pa, 1), stack(pb, 0), stack(pb, 1), stack(pb, 2),
            stack(sa, 0), stack(sa, 1), stack(sb, 0), stack(sb, 1), stack(sb, 2))
```

```python
import functools

import jax
import jax.numpy as jnp
from jax import lax
from jax.experimental import pallas as pl
from jax.experimental.pallas import tpu as pltpu

F32 = jnp.float32
BF16 = jnp.bfloat16

N_HEADS = 16
HEAD_DIM = 64
CHUNK = 64
N_PAST_CHUNKS = 8
A_PAST = N_PAST_CHUNKS * CHUNK
MAX_REL = 256
RMS_EPS = 1e-6
NEG_INF = -1e30
ATTN_SCALE = HEAD_DIM ** -0.5

LANES = 128
HEADS_PER_BLOCK = LANES // HEAD_DIM
VMEM_LIMIT_BYTES = 48 * 1024 * 1024

ROW_TILE = 512
A_TILE = 256
A_WIN = A_TILE + A_PAST
FOX_TILE = 512
FOX_QUERY_SPLIT = 2
FF_CHUNK = 256
REL_PAD = 384
G_WIDTH = 1024
SAMPLE_CUMSUM_TILE = 384


def _params(*sem):
    return pltpu.CompilerParams(dimension_semantics=sem, vmem_limit_bytes=VMEM_LIMIT_BYTES)


def _resident(shape):
    zeros = (0,) * len(shape)
    return pl.BlockSpec(shape, lambda *_: zeros, pipeline_mode=pl.Buffered(1))


def _nt_dot(a, b):
    return lax.dot_general(a, b, (((1,), (1,)), ((), ())), preferred_element_type=F32)


def _modulated_norm(x, g, scale, shift):
    ms = jnp.mean(x * x, axis=-1, keepdims=True)
    y = x * lax.rsqrt(ms + RMS_EPS) * g
    return y * (1.0 + scale) + shift


def _split3(x):
    hi = x.astype(BF16)
    r1 = x - hi.astype(F32)
    mid = r1.astype(BF16)
    lo = (r1 - mid.astype(F32)).astype(BF16)
    return hi, mid, lo


def _lower_lanes(shape):
    return lax.broadcasted_iota(jnp.int32, shape, len(shape) - 1) < HEAD_DIM


def _mod_body(c_ref, w_ref, b_ref, o_ref):
    cs = jax.nn.silu(c_ref[...]).astype(BF16)
    o_ref[0] = jnp.dot(cs, w_ref[0].astype(BF16), preferred_element_type=F32) + b_ref[0]


def _modulation(c, w_mod, b_mod):
    depth, d, n = w_mod.shape
    rows = c.shape[0]
    tn = 1536
    assert n % tn == 0 and rows % 16 == 0
    return pl.pallas_call(
        _mod_body,
        grid=(depth, n // tn),
        in_specs=[pl.BlockSpec((rows, d), lambda l, j: (0, 0)),
                  pl.BlockSpec((1, d, tn), lambda l, j: (l, 0, j)),
                  pl.BlockSpec((1, 1, tn), lambda l, j: (l, 0, j))],
        out_specs=pl.BlockSpec((1, rows, tn), lambda l, j: (l, 0, j)),
        out_shape=jax.ShapeDtypeStruct((depth, rows, n), F32),
        compiler_params=_params("parallel", "parallel"),
        name="modulation",
    )(c, w_mod, b_mod.reshape(depth, 1, n))


def _qkv_body(*refs, d, emit, n_unused, zero_other_slots_of):
    x_ref, mod_ref, g_ref, w_ref = refs[:4]
    pos = 4
    if "logf" in emit:
        wf_ref, bf_ref = refs[4:6]
        pos = 6
    out = dict(zip(emit, refs[pos + n_unused:]))
    if zero_other_slots_of is None:
        _qkv_compute(x_ref, mod_ref, g_ref, w_ref, refs[4:pos], out, d=d, emit=emit)
        return
    swept = pl.program_id(2)

    @pl.when(swept == zero_other_slots_of)
    def _():
        _qkv_compute(x_ref, mod_ref, g_ref, w_ref, refs[4:pos], out, d=d, emit=emit)

    @pl.when(swept != zero_other_slots_of)
    def _():
        for name in ("k32t", "v32t"):
            out[name][...] = jnp.zeros(out[name].shape, F32)


def _qkv_compute(x_ref, mod_ref, g_ref, w_ref, gate_refs, out, *, d, emit):
    if "logf" in emit:
        wf_ref, bf_ref = gate_refs
    x = x_ref[...]
    bb, ts, _ = x.shape
    h = _modulated_norm(x, g_ref[...], mod_ref[:, 1:2, :], mod_ref[:, 0:1, :])
    hb = h.reshape(bb * ts, d).astype(BF16)

    def put(name, val, dtype):
        out[name][...] = val.reshape(bb, ts, val.shape[-1]).astype(dtype)

    if "q" in emit:
        q = jnp.dot(hb, w_ref[:, 0:d], preferred_element_type=F32)
        put("q", q * ATTN_SCALE, BF16)
    k = jnp.dot(hb, w_ref[:, d:2 * d], preferred_element_type=F32)
    v = jnp.dot(hb, w_ref[:, 2 * d:3 * d], preferred_element_type=F32)
    if "k" in emit:
        put("k", k, BF16)
    if "v" in emit:
        put("v", v, BF16)
    if "vt" in emit or "v32t" in emit:
        v_t = v.T
    if "vt" in emit:
        out["vt"][0] = v_t.astype(BF16)
    if "v32t" in emit:
        out["v32t"][0, 0] = v_t
    if "k32t" in emit:
        out["k32t"][0, 0] = k.T
    if "k32" in emit:
        put("k32", k, F32)
    if "v32" in emit:
        put("v32", v, F32)
    if "logf" in emit:
        z = jnp.dot(hb, wf_ref[...], preferred_element_type=F32) + bf_ref[...]
        put("logf", jax.nn.log_sigmoid(z), F32)


def _qkv_proj(x, mod, g, w_qkv, wf=None, bf=None, *, bb, ts, emit, row_start=0, rows=None,
              state_slot=(0, 1), state=None):
    b, s, d = x.shape
    rows = s if rows is None else rows
    assert rows % ts == 0 and row_start % ts == 0 and b % bb == 0
    assert ts % 16 == 0 or bb == 1
    assert bb == 1 or not {"vt", "k32t", "v32t"} & set(emit)
    off = row_start // ts
    slot, n_slots = state_slot
    assert ("logf" in emit) == (wf is not None)
    sweep = state is None and n_slots > 1 and bool({"k32t", "v32t"} & set(emit))
    row_spec = lambda width: pl.BlockSpec((bb, ts, width), lambda i, j, *_: (i, j, 0))
    in_specs = [pl.BlockSpec((bb, ts, d), lambda i, j, *_: (i, j + off, 0)),
                pl.BlockSpec((bb, 6, d), lambda i, j, *_: (i, 0, 0)),
                _resident((1, d)),
                _resident((d, 3 * d))]
    args = [x, mod, g.reshape(1, d), w_qkv]
    if wf is not None:
        in_specs += [_resident((d, LANES)), _resident((1, LANES))]
        args += [wf, bf]
    kinds = {
        "q": (row_spec(d), (b, rows, d), BF16), "k": (row_spec(d), (b, rows, d), BF16),
        "v": (row_spec(d), (b, rows, d), BF16),
        "vt": (pl.BlockSpec((1, d, ts), lambda i, j, *_: (i, 0, j)), (b, d, rows), BF16),
        "k32": (row_spec(d), (b, rows, d), F32), "v32": (row_spec(d), (b, rows, d), F32),
        "logf": (row_spec(LANES), (b, rows, LANES), F32),
    }
    state_spec = pl.BlockSpec((1, 1, d, ts),
                              lambda i, j, *swept: (swept[0] if swept else slot, i, 0, j))
    kinds["k32t"] = kinds["v32t"] = (state_spec, (n_slots, b, d, rows), F32)
    aliases = {}
    if state is not None:
        for name in ("k32t", "v32t"):
            aliases[len(args)] = emit.index(name)
            in_specs.append(pl.BlockSpec(memory_space=pl.ANY))
            args.append(state[name])
    outs = pl.pallas_call(
        functools.partial(_qkv_body, d=d, emit=emit, n_unused=len(aliases),
                          zero_other_slots_of=slot if sweep else None),
        grid=(b // bb, rows // ts) + ((n_slots,) if sweep else ()),
        in_specs=in_specs,
        out_specs=[kinds[e][0] for e in emit],
        out_shape=[jax.ShapeDtypeStruct(kinds[e][1], kinds[e][2]) for e in emit],
        input_output_aliases=aliases,
        compiler_params=_params(*(("parallel", "parallel") + (("arbitrary",) if sweep else ()))),
        name="norm_qkv",
    )(*args)
    return dict(zip(emit, outs))


def _post_attn_body(*refs, d_ff, final_norm):
    o_ref, x_ref, mod_ref, wo_ref, g_ref, wgu_ref, wd_ref = refs[:7]
    out_ref = refs[-1]
    x = x_ref[...]
    bb, ts, d = x.shape
    mod = lambda r: mod_ref[:, r:r + 1, :]
    y = jnp.dot(o_ref[...].reshape(bb * ts, d), wo_ref[...], preferred_element_type=F32)
    x1 = x + mod(2) * y.reshape(bb, ts, d)
    h = _modulated_norm(x1, g_ref[...], mod(4), mod(3))
    hb = h.reshape(bb * ts, d).astype(BF16)
    acc = jnp.zeros((bb * ts, d), F32)
    for c in range(d_ff // FF_CHUNK):
        lo = c * FF_CHUNK
        gate = jnp.dot(hb, wgu_ref[:, lo:lo + FF_CHUNK], preferred_element_type=F32)
        up = jnp.dot(hb, wgu_ref[:, d_ff + lo:d_ff + lo + FF_CHUNK], preferred_element_type=F32)
        act = (jax.nn.silu(gate) * up).astype(BF16)
        acc = acc + jnp.dot(act, wd_ref[lo:lo + FF_CHUNK, :], preferred_element_type=F32)
    x2 = x1 + mod(5) * acc.reshape(bb, ts, d)
    if final_norm:
        gout_ref = refs[7]
        ms = jnp.mean(x2 * x2, axis=-1, keepdims=True)
        x2 = x2 * lax.rsqrt(ms + RMS_EPS) * gout_ref[...]
    out_ref[...] = x2


def _post_attn(o, x, mod, w_o, g, w_gu, w_down, g_out=None, *, bb, ts):
    b, s, d = x.shape
    d_ff = w_down.shape[0]
    assert s % ts == 0 and b % bb == 0 and d_ff % FF_CHUNK == 0
    assert ts % 16 == 0 or bb == 1
    row = pl.BlockSpec((bb, ts, d), lambda i, j: (i, j, 0))
    in_specs = [row, row, pl.BlockSpec((bb, 6, d), lambda i, j: (i, 0, 0)), _resident((d, d)),
                _resident((1, d)), _resident((d, 2 * d_ff)), _resident((d_ff, d))]
    args = [o, x, mod, w_o, g.reshape(1, d), w_gu, w_down]
    if g_out is not None:
        in_specs.append(_resident((1, d)))
        args.append(g_out.reshape(1, d))
    return pl.pallas_call(
        functools.partial(_post_attn_body, d_ff=d_ff, final_norm=g_out is not None),
        grid=(b // bb, s // ts),
        in_specs=in_specs, out_specs=row,
        out_shape=jax.ShapeDtypeStruct((b, s, d), F32),
        compiler_params=_params("parallel", "parallel"),
        name="oproj_ffn",
    )(*args)


def _bias_row_body(tab_ref, g_ref):
    u = lax.broadcasted_iota(jnp.int32, (REL_PAD, G_WIDTH), 1)
    r = lax.broadcasted_iota(jnp.int32, (REL_PAD, G_WIDTH), 0)
    rel = A_PAST + (A_TILE - 1) - u
    idx = jnp.clip(rel, -(CHUNK - 1), MAX_REL) + (CHUNK - 1)
    onehot = jnp.where(r == idx, 1.0, 0.0).astype(BF16)
    hi, mid, lo = _split3(tab_ref[...])
    g_ref[...] = (jnp.dot(hi, onehot, preferred_element_type=F32)
                  + jnp.dot(mid, onehot, preferred_element_type=F32)
                  + jnp.dot(lo, onehot, preferred_element_type=F32))


def _bias_tile_body(g_ref, o_ref, ot_ref):
    x = jnp.broadcast_to(g_ref[0], (A_TILE, G_WIDTH))
    rolled = pltpu.roll(x, G_WIDTH - (A_TILE - 1), 1, stride=1, stride_axis=0)
    val = rolled[:, :A_WIN]
    qc = lax.broadcasted_iota(jnp.int32, (A_TILE, A_WIN), 0) // CHUNK
    kc = lax.broadcasted_iota(jnp.int32, (A_TILE, A_WIN), 1) // CHUNK
    band = (kc >= qc) & (kc <= qc + N_PAST_CHUNKS)
    tile = jnp.where(band, val, NEG_INF)
    o_ref[0] = tile
    ot_ref[0] = tile.T


def _rel_bias_tile(table):
    h, rel = table.shape
    assert rel == CHUNK + MAX_REL and rel <= REL_PAD and A_WIN + A_TILE - 1 <= G_WIDTH
    tab = jnp.pad(table, ((0, 0), (0, REL_PAD - rel)))
    gen = pl.pallas_call(
        _bias_row_body,
        out_shape=jax.ShapeDtypeStruct((h, G_WIDTH), F32),
        name="rel_bias_row",
    )(tab)
    return pl.pallas_call(
        _bias_tile_body,
        grid=(h,),
        in_specs=[pl.BlockSpec((1, 1, G_WIDTH), lambda i: (i, 0, 0))],
        out_specs=[pl.BlockSpec((1, A_TILE, A_WIN), lambda i: (i, 0, 0)),
                   pl.BlockSpec((1, A_WIN, A_TILE), lambda i: (i, 0, 0))],
        out_shape=[jax.ShapeDtypeStruct((h, A_TILE, A_WIN), F32),
                   jax.ShapeDtypeStruct((h, A_WIN, A_TILE), F32)],
        compiler_params=_params("parallel"),
        name="rel_bias_tile",
    )(gen.reshape(h, 1, G_WIDTH))


def _head_select(shape, hh):
    lower = _lower_lanes(shape)
    return lower if hh == 0 else jnp.logical_not(lower)


def _chunk_attn_body(q_ref, k_ref, vt_ref, bias_ref, o_ref, s0_ref, s1_ref, m0_ref, m1_ref,
                     *, n_tiles):
    sets = ((s0_ref, m0_ref), (s1_ref, m1_ref))
    lower = _lower_lanes((A_TILE, LANES))

    def head_queries(q):
        zero = jnp.zeros_like(q)
        return jnp.where(lower, q, zero), jnp.where(lower, zero, q)

    def head_values(vt):
        ones = jnp.ones((HEAD_DIM, vt.shape[1]), BF16)
        return (jnp.concatenate([vt[:HEAD_DIM], ones], axis=0),
                jnp.concatenate([ones, vt[HEAD_DIM:]], axis=0))

    def finish(accs):
        o0 = accs[0][:HEAD_DIM] / accs[0][HEAD_DIM:]
        o1 = accs[1][HEAD_DIM:] / accs[1][:HEAD_DIM]
        return jnp.concatenate([o0, o1], axis=0).T.astype(o_ref.dtype)

    def edge_tile(t):
        nk = (t + 1) * A_TILE
        qm = head_queries(q_ref[0, t * A_TILE:(t + 1) * A_TILE, :])
        kw = k_ref[0, 0:nk, :]
        vh = head_values(vt_ref[0, :, 0:nk])
        accs = []
        for hh in range(HEADS_PER_BLOCK):
            s = _nt_dot(kw, qm[hh]) + bias_ref[hh, A_WIN - nk:A_WIN, :]
            pt = jnp.exp(s - jnp.max(s, axis=0, keepdims=True)).astype(BF16)
            accs.append(jnp.dot(vh[hh], pt, preferred_element_type=F32))
        o_ref[0, t * A_TILE:(t + 1) * A_TILE, :] = finish(accs)

    def score_step(t, dst):
        s_ref, m_ref = sets[dst]
        qm = head_queries(q_ref[0, pl.ds(pl.multiple_of(t * A_TILE, A_TILE), A_TILE), :])
        kw = k_ref[0, pl.ds(pl.multiple_of((t - 2) * A_TILE, A_TILE), A_WIN), :]
        for hh in range(HEADS_PER_BLOCK):
            s = _nt_dot(kw, qm[hh]) + bias_ref[hh]
            s_ref[hh] = s
            m_ref[hh] = jnp.max(s, axis=0, keepdims=True)

    def value_step(t, src):
        s_ref, m_ref = sets[src]
        vh = head_values(vt_ref[0, :, pl.ds(pl.multiple_of((t - 2) * A_TILE, A_TILE), A_WIN)])
        accs = []
        for hh in range(HEADS_PER_BLOCK):
            pt = jnp.exp(s_ref[hh] - m_ref[hh]).astype(BF16)
            accs.append(jnp.dot(vh[hh], pt, preferred_element_type=F32))
        o_ref[0, pl.ds(pl.multiple_of(t * A_TILE, A_TILE), A_TILE), :] = finish(accs)

    edge_tile(0)
    edge_tile(1)
    score_step(2, 0)

    def two_tiles(u, carry):
        t = 2 + 2 * u
        score_step(t + 1, 1)
        value_step(t, 0)
        score_step(t + 2, 0)
        value_step(t + 1, 1)
        return carry

    lax.fori_loop(0, (n_tiles - 4) // 2, two_tiles, 0)
    score_step(n_tiles - 1, 1)
    value_step(n_tiles - 2, 0)
    value_step(n_tiles - 1, 1)


def _chunk_attn_prompt(q, k, vt, bias_t):
    b, s, d = q.shape
    n_tiles = s // A_TILE
    assert A_WIN == 3 * A_TILE and s % A_TILE == 0 and d % LANES == 0
    assert n_tiles >= 4 and n_tiles % 2 == 0
    pairs = d // LANES
    rows = pl.BlockSpec((1, s, LANES), lambda p, i: (i, 0, p))
    return pl.pallas_call(
        functools.partial(_chunk_attn_body, n_tiles=n_tiles),
        grid=(pairs, b),
        in_specs=[rows, rows, pl.BlockSpec((1, LANES, s), lambda p, i: (i, p, 0)),
                  pl.BlockSpec((HEADS_PER_BLOCK, A_WIN, A_TILE), lambda p, i: (p, 0, 0))],
        out_specs=rows,
        out_shape=jax.ShapeDtypeStruct((b, s, d), BF16),
        scratch_shapes=[pltpu.VMEM((HEADS_PER_BLOCK, A_WIN, A_TILE), F32)] * 2
        + [pltpu.VMEM((HEADS_PER_BLOCK, 1, A_TILE), F32)] * 2,
        compiler_params=_params("parallel", "parallel"),
        name="chunk_attn",
    )(q, k, vt, bias_t)


def _sample_attn_body(*refs, n_cache, forget):
    q_ref, kc_ref, vc_ref, kn_ref, vn_ref = refs[:5]
    o_ref = refs[-1]
    t, d = q_ref.shape[1:]
    if forget:
        fq_ref, fk_ref = refs[5:7]
        causal = (lax.broadcasted_iota(jnp.int32, (t, t), 1)
                  <= lax.broadcasted_iota(jnp.int32, (t, t), 0))
    else:
        bias_ref = refs[5]
    for p in range(d // LANES):
        lanes = slice(p * LANES, (p + 1) * LANES)
        q = q_ref[0, :, lanes]
        kc_t = kc_ref[0, 0, lanes, :].astype(BF16)
        vc_t = vc_ref[0, 0, lanes, :].astype(BF16)
        kn = kn_ref[0, :, lanes]
        vn = vn_ref[0, :, lanes]
        outs = []
        for hh in range(HEADS_PER_BLOCK):
            head = HEADS_PER_BLOCK * p + hh
            qm = jnp.where(_head_select(q.shape, hh), q, jnp.zeros_like(q))
            if forget:
                fq = fq_ref[0, :, head:head + 1]
                sc = (jnp.dot(qm, kc_t, preferred_element_type=F32)
                      + (fq - fk_ref[0, head:head + 1, 0:n_cache]))
                sn = _nt_dot(qm, kn) + (fq - fk_ref[0, head:head + 1, n_cache:n_cache + t])
                sn = jnp.where(causal, sn, NEG_INF)
            else:
                sc = (jnp.dot(qm, kc_t, preferred_element_type=F32)
                      + bias_ref[head, :, 0:n_cache])
                sn = _nt_dot(qm, kn) + bias_ref[head, :, n_cache:n_cache + t]
            m = jnp.maximum(jnp.max(sc, axis=-1, keepdims=True),
                            jnp.max(sn, axis=-1, keepdims=True))
            pc = jnp.exp(sc - m)
            pn = jnp.exp(sn - m)
            l = jnp.sum(pc, axis=-1, keepdims=True) + jnp.sum(pn, axis=-1, keepdims=True)
            o = (_nt_dot(pc.astype(BF16), vc_t)
                 + jnp.dot(pn.astype(BF16), vn, preferred_element_type=F32))
            outs.append(o / l)
        o_ref[0, :, lanes] = jnp.where(_lower_lanes(outs[0].shape), outs[0],
                                       outs[1]).astype(o_ref.dtype)


def _sample_attn(q, k, v, k_cache, v_cache, layer, *, bias=None, f_col_new=None, f_row=None):
    b, t, d = q.shape
    n_cache = k_cache.shape[3]
    new = pl.BlockSpec((1, t, d), lambda i: (i, 0, 0))
    cache = pl.BlockSpec((1, 1, d, n_cache), lambda i: (layer, i, 0, 0))
    forget = bias is None
    if forget:
        extra_specs = [pl.BlockSpec((1, t, LANES), lambda i: (i, 0, 0)),
                       pl.BlockSpec((1,) + f_row.shape[1:], lambda i: (i, 0, 0))]
        extra = [f_col_new, f_row]
    else:
        assert n_cache == A_PAST and t <= CHUNK and n_cache + t <= A_WIN
        extra_specs = [pl.BlockSpec((bias.shape[0], t, A_WIN), lambda i: (0, 0, 0))]
        extra = [bias]
    return pl.pallas_call(
        functools.partial(_sample_attn_body, n_cache=n_cache, forget=forget),
        grid=(b,),
        in_specs=[new, cache, cache, new, new] + extra_specs,
        out_specs=new,
        out_shape=jax.ShapeDtypeStruct((b, t, d), BF16),
        compiler_params=_params("parallel"),
        name="sample_attn",
    )(q, k_cache, v_cache, k, v, *extra)


def _bias_lanes(f, d):
    pieces = jnp.concatenate(_split3(f), axis=1)
    col = lax.broadcasted_iota(jnp.int32, (3 * LANES, d), 1)
    row = lax.broadcasted_iota(jnp.int32, (3 * LANES, d), 0)
    lane = col % LANES
    head = HEADS_PER_BLOCK * (col // LANES) + jnp.where(lane >= HEAD_DIM, 0, 1)
    slot = lane % HEAD_DIM
    hit = (row % LANES) == head

    def pick(first):
        sel = jnp.where(hit & (slot == first + row // LANES), 1.0, 0.0).astype(BF16)
        return jnp.dot(pieces, sel, preferred_element_type=F32)

    slot_row = lax.broadcasted_iota(jnp.int32, (1, d), 1) % HEAD_DIM
    ones = lambda first: jnp.where((slot_row >= first) & (slot_row < first + 3), 1.0, 0.0)
    q_side = pick(0) + ones(3)
    k_side = ones(0) - pick(3)
    return q_side.astype(BF16), k_side.astype(BF16)


def _cumsum_body(x_ref, *refs, tc, n_heads, d, emit_bias_lanes):
    carry_ref = refs[-1]

    @pl.when(pl.program_id(1) == 0)
    def _():
        carry_ref[...] = jnp.zeros_like(carry_ref)

    r = lax.broadcasted_iota(jnp.int32, (tc, tc), 0)
    c = lax.broadcasted_iota(jnp.int32, (tc, tc), 1)
    tri = jnp.where(c <= r, 1.0, 0.0).astype(BF16)
    pieces = jnp.concatenate(_split3(x_ref[0]), axis=1)
    sums = jnp.dot(tri, pieces, preferred_element_type=F32)
    f = ((sums[:, 0:LANES] + sums[:, LANES:2 * LANES]) + sums[:, 2 * LANES:3 * LANES]
         + carry_ref[0:1, :])
    carry_ref[0:1, :] = f[tc - 1:tc, :]
    if emit_bias_lanes:
        qa_ref, ka_ref = refs[:2]
        qa_ref[0], ka_ref[0] = _bias_lanes(f, d)
    else:
        fcol_ref, frow_ref = refs[:2]
        fcol_ref[0] = f
        frow_ref[0] = f.T[0:n_heads, :]


def _cumsum(logf, *, tc, d, emit_bias_lanes, n_heads=N_HEADS):
    b, s, w = logf.shape
    assert s % tc == 0 and tc % LANES == 0 and w == LANES
    if emit_bias_lanes:
        out_specs = [pl.BlockSpec((1, tc, d), lambda i, j: (i, j, 0))] * 2
        out_shape = [jax.ShapeDtypeStruct((b, s, d), BF16)] * 2
    else:
        out_specs = [pl.BlockSpec((1, tc, LANES), lambda i, j: (i, j, 0)),
                     pl.BlockSpec((1, n_heads, tc), lambda i, j: (i, 0, j))]
        out_shape = [jax.ShapeDtypeStruct((b, s, LANES), F32),
                     jax.ShapeDtypeStruct((b, n_heads, s), F32)]
    return pl.pallas_call(
        functools.partial(_cumsum_body, tc=tc, n_heads=n_heads, d=d,
                          emit_bias_lanes=emit_bias_lanes),
        grid=(b, s // tc),
        in_specs=[pl.BlockSpec((1, tc, LANES), lambda i, j: (i, j, 0))],
        out_specs=out_specs, out_shape=out_shape,
        scratch_shapes=[pltpu.VMEM((8, LANES), F32)],
        compiler_params=_params("parallel", "arbitrary"),
        name="logf_cumsum",
    )(logf)


def _fox_body(q_ref, qa_ref, k_ref, ka_ref, vt_ref, o_ref, qh_ref, acc_ref,
              s0_ref, s1_ref, m0_ref, m1_ref, a0_ref, a1_ref, *, tile):
    i = pl.program_id(2)
    bufs = ((s0_ref, m0_ref, a0_ref), (s1_ref, m1_ref, a1_ref))
    qw = tile // FOX_QUERY_SPLIT
    chains = [(hh, qs) for hh in range(HEADS_PER_BLOCK) for qs in range(FOX_QUERY_SPLIT)]
    lower = _lower_lanes((tile, LANES))
    q = q_ref[0]
    qa = qa_ref[0]
    qh_ref[0] = jnp.where(lower, q, qa)
    qh_ref[1] = jnp.where(lower, qa, q)
    acc_ref[...] = jnp.zeros_like(acc_ref)
    ones = jnp.ones((HEAD_DIM, tile), BF16)

    def score_step(j, dst, src, diagonal=False):
        s_ref, m_ref, a_ref = bufs[dst]
        off = pl.multiple_of(j * tile, tile)
        kb = k_ref[0, pl.ds(off, tile), :]
        ka = ka_ref[0, pl.ds(off, tile), :]
        kh = (jnp.where(lower, kb, ka), jnp.where(lower, ka, kb))
        for c, (hh, qs) in enumerate(chains):
            s = _nt_dot(kh[hh], qh_ref[hh, qs * qw:(qs + 1) * qw, :])
            if diagonal:
                key = lax.broadcasted_iota(jnp.int32, s.shape, 0)
                qry = lax.broadcasted_iota(jnp.int32, s.shape, 1) + qs * qw
                s = jnp.where(key <= qry, s, NEG_INF)
            m_prev = bufs[src][1][c]
            m_new = jnp.maximum(m_prev, jnp.max(s, axis=0, keepdims=True))
            s_ref[c] = s
            m_ref[c] = m_new
            a_ref[c] = jnp.exp(m_prev - m_new)

    def value_step(j, src):
        s_ref, m_ref, a_ref = bufs[src]
        off = pl.multiple_of(j * tile, tile)
        vt = vt_ref[0, :, pl.ds(off, tile)]
        vh = (jnp.concatenate([vt[:HEAD_DIM], ones], axis=0),
              jnp.concatenate([ones, vt[HEAD_DIM:]], axis=0))
        for c, (hh, qs) in enumerate(chains):
            pt = jnp.exp(s_ref[c] - m_ref[c]).astype(BF16)
            cols = slice(qs * qw, (qs + 1) * qw)
            acc_ref[hh, :, cols] = (a_ref[c] * acc_ref[hh, :, cols]
                                    + jnp.dot(vh[hh], pt, preferred_element_type=F32))

    block_of_step = lambda t: jnp.where(t == 0, i, t - 1)
    m1_ref[...] = jnp.full(m1_ref.shape, NEG_INF, F32)
    score_step(i, 0, 1, diagonal=True)

    def two_steps(pp):
        t = 2 * pp
        score_step(t, 1, 0)
        value_step(block_of_step(t), 0)
        score_step(t + 1, 0, 1)
        value_step(t, 1)

    def four_steps(u, carry):
        two_steps(2 * u)
        two_steps(2 * u + 1)
        return carry

    pairs_of_steps = i // 2
    lax.fori_loop(0, pairs_of_steps // 2, four_steps, 0)

    @pl.when(pairs_of_steps % 2 == 1)
    def _():
        two_steps(pairs_of_steps - 1)

    @pl.when(i % 2 == 1)
    def _():
        score_step(i - 1, 1, 0)
        value_step(block_of_step(i - 1), 0)
        value_step(i - 1, 1)

    @pl.when(i % 2 == 0)
    def _():
        value_step(block_of_step(i), 0)

    o0 = acc_ref[0, :HEAD_DIM, :] / acc_ref[0, HEAD_DIM:, :]
    o1 = acc_ref[1, HEAD_DIM:, :] / acc_ref[1, :HEAD_DIM, :]
    o_ref[0] = jnp.concatenate([o0, o1], axis=0).T.astype(o_ref.dtype)


def _fox_attn_prompt(q, qa, k, ka, vt):
    b, s, d = q.shape
    tile = FOX_TILE
    assert s % tile == 0
    pairs = d // LANES
    n_chains = HEADS_PER_BLOCK * FOX_QUERY_SPLIT
    qw = tile // FOX_QUERY_SPLIT
    qblk = pl.BlockSpec((1, tile, LANES), lambda i, p, t: (i, t, p))
    seq = pl.BlockSpec((1, s, LANES), lambda i, p, t: (i, 0, p))
    return pl.pallas_call(
        functools.partial(_fox_body, tile=tile),
        grid=(b, pairs, s // tile),
        in_specs=[qblk, qblk, seq, seq, pl.BlockSpec((1, LANES, s), lambda i, p, t: (i, p, 0))],
        out_specs=qblk,
        out_shape=jax.ShapeDtypeStruct((b, s, d), BF16),
        scratch_shapes=[pltpu.VMEM((HEADS_PER_BLOCK, tile, LANES), BF16),
                        pltpu.VMEM((HEADS_PER_BLOCK, LANES, tile), F32)]
        + [pltpu.VMEM((n_chains, tile, qw), F32)] * 2
        + [pltpu.VMEM((n_chains, 1, qw), F32)] * 4,
        compiler_params=_params("parallel", "parallel", "parallel"),
        name="fox_attn",
    )(q, qa, k, ka, vt)


def _trunk(x, mods, weights, a_mix, b_mix, *, bb, ts, keep, v_name):
    b, s, d = x.shape
    depth = mods.shape[0]
    n_a, n_b = (depth + 1) // 2, depth // 2
    transposed = v_name == "vt"
    k_name, v32_name = ("k32t", "v32t") if transposed else ("k32", "v32")
    a_state, b_state, b_logf = None, None, []
    stacked = lambda st: {k_name: jnp.stack([e[k_name] for e in st]),
                          v32_name: jnp.stack([e[v32_name] for e in st])}
    a_list, b_list = [], []
    for l in range(depth):
        mod = mods[l]
        g, w = weights["g_mix"][l], weights["w_qkv"][l]
        if l % 2 == 0:
            kw = dict(state_slot=(l // 2, n_a), state=a_state) if transposed else {}
            if keep == s:
                proj = kv32 = _qkv_proj(x, mod, g, w, bb=bb, ts=ts,
                                        emit=("q", "k", v_name, k_name, v32_name), **kw)
            else:
                proj = _qkv_proj(x, mod, g, w, bb=bb, ts=ts, emit=("q", "k", v_name))
                kv32 = _qkv_proj(x, mod, g, w, bb=bb, ts=min(ts, keep), emit=(k_name, v32_name),
                                 row_start=s - keep, rows=keep, **kw)
            a_state = kv32
            a_list.append(kv32)
            o = a_mix(l // 2, proj)
        else:
            ib = l // 2
            kw = dict(state_slot=(ib, n_b), state=b_state) if transposed else {}
            proj = _qkv_proj(x, mod, g, w, weights["w_fgate"][ib], weights["b_fgate"][ib],
                             bb=bb, ts=ts, emit=("q", "k", v_name, k_name, v32_name, "logf"), **kw)
            b_state = proj
            b_list.append(proj)
            b_logf.append(proj["logf"][:, :, :N_HEADS])
            o = b_mix(ib, proj)
        g_out = weights["g_out"] if l == depth - 1 else None
        x = _post_attn(o, x, mod, weights["w_o"][l], weights["g_ffn"][l], weights["w_gu"][l],
                       weights["w_down"][l], g_out, bb=bb, ts=ts)
    if not transposed:
        a_state, b_state = stacked(a_list), stacked(b_list)

    def heads(a):
        if transposed:
            n, bsz, _, r = a.shape
            return a.reshape(n, bsz, N_HEADS, HEAD_DIM, r).transpose(0, 1, 4, 2, 3)
        return a.reshape(a.shape[:3] + (N_HEADS, HEAD_DIM))

    return (x, heads(a_state[k_name]), heads(a_state[v32_name]),
            heads(b_state[k_name]), heads(b_state[v32_name]), jnp.stack(b_logf))


def kernel(x_prompt, x_sample, cache_a_k, cache_a_v, cache_b_k, cache_b_v, cache_b_logf,
           c_prompt, c_sample, w_mod, b_mod, g_mix, g_ffn, w_qkv, w_o, rel_bias,
           w_fgate, b_fgate, w_gu, w_down, g_out):
    bp, sp, d = x_prompt.shape
    bs, t_new, _ = x_sample.shape
    depth = w_mod.shape[0]
    n_b = w_fgate.shape[0]
    assert d == N_HEADS * HEAD_DIM
    keep = min(A_PAST, sp)

    weights = {
        "g_mix": g_mix, "g_ffn": g_ffn, "g_out": g_out,
        "w_qkv": w_qkv.astype(BF16), "w_o": w_o.astype(BF16),
        "w_gu": w_gu.astype(BF16), "w_down": w_down.astype(BF16),
        "w_fgate": jnp.pad(w_fgate, ((0, 0), (0, 0), (0, LANES - N_HEADS))).astype(BF16),
        "b_fgate": jnp.pad(b_fgate, ((0, 0), (0, LANES - N_HEADS))).reshape(n_b, 1, LANES),
    }

    c_all = jnp.concatenate([c_prompt, c_sample], axis=0)
    rows = -(-c_all.shape[0] // 16) * 16
    c_all = jnp.pad(c_all, ((0, rows - c_all.shape[0]), (0, 0)))
    mods = _modulation(c_all, w_mod, b_mod)
    mods_p = mods[:, :bp].reshape(depth, bp, 6, d)
    mods_s = mods[:, bp:bp + bs].reshape(depth, bs, 6, d)

    bias = [_rel_bias_tile(rel_bias[i]) for i in range(rel_bias.shape[0])]

    def a_prompt(i, proj):
        return _chunk_attn_prompt(proj["q"], proj["k"], proj["vt"], bias[i][1])

    def b_prompt(i, proj):
        qa, ka = _cumsum(proj["logf"], tc=FOX_TILE, d=d, emit_bias_lanes=True)
        return _fox_attn_prompt(proj["q"], qa, proj["k"], ka, proj["vt"])

    prompt_out = _trunk(x_prompt, mods_p, weights, a_prompt, b_prompt, bb=1, ts=ROW_TILE,
                              keep=keep, v_name="vt")

    by_feature = lambda c: c.transpose(0, 1, 3, 4, 2).reshape(c.shape[0], bs, d, c.shape[2])
    ca_k, ca_v = by_feature(cache_a_k), by_feature(cache_a_v)
    cb_k, cb_v = by_feature(cache_b_k), by_feature(cache_b_v)
    past = cache_b_k.shape[2]
    total = -(-(past + t_new) // SAMPLE_CUMSUM_TILE) * SAMPLE_CUMSUM_TILE

    def a_sample(i, proj):
        return _sample_attn(proj["q"], proj["k"], proj["v"], ca_k, ca_v, i, bias=bias[i][0])

    def b_sample(i, proj):
        lf_cache = jnp.pad(cache_b_logf[i].astype(F32), ((0, 0), (0, 0), (0, LANES - N_HEADS)))
        lf = jnp.concatenate([lf_cache, proj["logf"]], axis=1)
        lf = jnp.pad(lf, ((0, 0), (0, total - (past + t_new)), (0, 0)))
        f_col, f_row = _cumsum(lf, tc=SAMPLE_CUMSUM_TILE, d=d, emit_bias_lanes=False)
        return _sample_attn(proj["q"], proj["k"], proj["v"], cb_k, cb_v, i,
                            f_col_new=f_col[:, past:past + t_new], f_row=f_row)

    sample_out = _trunk(x_sample, mods_s, weights, a_sample, b_sample, bb=bs, ts=t_new,
                              keep=t_new, v_name="v")

    return (prompt_out[0], sample_out[0]) + tuple(prompt_out[1:]) + tuple(sample_out[1:])
```

```python
import functools

import jax
import jax.numpy as jnp
from jax import lax
from jax.experimental import pallas as pl
from jax.experimental.pallas import tpu as pltpu

F32 = jnp.float32
BF16 = jnp.bfloat16

N_HEADS = 16
HEAD_DIM = 64
CHUNK = 64
N_PAST_CHUNKS = 8
A_PAST = N_PAST_CHUNKS * CHUNK
MAX_REL = 256
RMS_EPS = 1e-6
NEG_INF = -1e30
ATTN_SCALE = HEAD_DIM ** -0.5

LANES = 128
HEADS_PER_BLOCK = LANES // HEAD_DIM
VMEM_LIMIT_BYTES = 48 * 1024 * 1024

ROW_TILE = 512
A_TILE = 256
A_WIN = A_TILE + A_PAST
FOX_TILE = 512
FOX_QUERY_SPLIT = 2
FF_CHUNK = 256
REL_PAD = 384
G_WIDTH = 1024
SAMPLE_CUMSUM_TILE = 384


def _params(*sem):
    return pltpu.CompilerParams(dimension_semantics=sem, vmem_limit_bytes=VMEM_LIMIT_BYTES)


def _resident(shape):
    zeros = (0,) * len(shape)
    return pl.BlockSpec(shape, lambda *_: zeros, pipeline_mode=pl.Buffered(1))


def _nt_dot(a, b):
    return lax.dot_general(a, b, (((1,), (1,)), ((), ())), preferred_element_type=F32)


def _modulated_norm(x, g, scale, shift):
    ms = jnp.mean(x * x, axis=-1, keepdims=True)
    y = x * lax.rsqrt(ms + RMS_EPS) * g
    return y * (1.0 + scale) + shift


def _split3(x):
    hi = x.astype(BF16)
    r1 = x - hi.astype(F32)
    mid = r1.astype(BF16)
    lo = (r1 - mid.astype(F32)).astype(BF16)
    return hi, mid, lo


def _lower_lanes(shape):
    return lax.broadcasted_iota(jnp.int32, shape, len(shape) - 1) < HEAD_DIM


def _mod_body(c_ref, w_ref, b_ref, o_ref):
    cs = jax.nn.silu(c_ref[...]).astype(BF16)
    o_ref[0] = jnp.dot(cs, w_ref[0].astype(BF16), preferred_element_type=F32) + b_ref[0]


def _modulation(c, w_mod, b_mod):
    depth, d, n = w_mod.shape
    rows = c.shape[0]
    tn = 1536
    assert n % tn == 0 and rows % 16 == 0
    return pl.pallas_call(
        _mod_body,
        grid=(depth, n // tn),
        in_specs=[pl.BlockSpec((rows, d), lambda l, j: (0, 0)),
                  pl.BlockSpec((1, d, tn), lambda l, j: (l, 0, j)),
                  pl.BlockSpec((1, 1, tn), lambda l, j: (l, 0, j))],
        out_specs=pl.BlockSpec((1, rows, tn), lambda l, j: (l, 0, j)),
        out_shape=jax.ShapeDtypeStruct((depth, rows, n), F32),
        compiler_params=_params("parallel", "parallel"),
        name="modulation",
    )(c, w_mod, b_mod.reshape(depth, 1, n))


def _qkv_body(*refs, d, emit, n_prev):
    x_ref, mod_ref, g_ref, w_ref = refs[:4]
    pos = 4
    if "logf" in emit:
        wf_ref, bf_ref = refs[4:6]
        pos = 6
    prev = dict(zip(("k32t", "v32t"), refs[pos:pos + (2 if n_prev else 0)]))
    out = dict(zip(emit, refs[pos + len(prev):]))
    for name, prev_ref in prev.items():
        out[name][0:n_prev] = prev_ref[...]
    x = x_ref[...]
    bb, ts, _ = x.shape
    h = _modulated_norm(x, g_ref[...], mod_ref[:, 1:2, :], mod_ref[:, 0:1, :])
    hb = h.reshape(bb * ts, d).astype(BF16)

    def put(name, val, dtype):
        out[name][...] = val.reshape(bb, ts, val.shape[-1]).astype(dtype)

    if "q" in emit:
        q = jnp.dot(hb, w_ref[:, 0:d], preferred_element_type=F32)
        put("q", q * ATTN_SCALE, BF16)
    k = jnp.dot(hb, w_ref[:, d:2 * d], preferred_element_type=F32)
    v = jnp.dot(hb, w_ref[:, 2 * d:3 * d], preferred_element_type=F32)
    if "k" in emit:
        put("k", k, BF16)
    if "v" in emit:
        put("v", v, BF16)
    if "vt" in emit or "v32t" in emit:
        v_t = v.T
    if "vt" in emit:
        out["vt"][0] = v_t.astype(BF16)
    if "v32t" in emit:
        out["v32t"][n_prev, 0] = v_t
    if "k32t" in emit:
        out["k32t"][n_prev, 0] = k.T
    if "k32" in emit:
        put("k32", k, F32)
    if "v32" in emit:
        put("v32", v, F32)
    if "logf" in emit:
        z = jnp.dot(hb, wf_ref[...], preferred_element_type=F32) + bf_ref[...]
        put("logf", jax.nn.log_sigmoid(z), F32)


def _qkv_proj(x, mod, g, w_qkv, wf=None, bf=None, *, bb, ts, emit, row_start=0, rows=None,
              state=None):
    b, s, d = x.shape
    rows = s if rows is None else rows
    assert rows % ts == 0 and row_start % ts == 0 and b % bb == 0
    assert ts % 16 == 0 or bb == 1
    assert bb == 1 or not {"vt", "k32t", "v32t"} & set(emit)
    off = row_start // ts
    n_prev = 0 if state is None else state["k32t"].shape[0]
    assert ("logf" in emit) == (wf is not None)
    row_spec = lambda width: pl.BlockSpec((bb, ts, width), lambda i, j: (i, j, 0))
    in_specs = [pl.BlockSpec((bb, ts, d), lambda i, j: (i, j + off, 0)),
                pl.BlockSpec((bb, 6, d), lambda i, j: (i, 0, 0)),
                _resident((1, d)),
                _resident((d, 3 * d))]
    args = [x, mod, g.reshape(1, d), w_qkv]
    if wf is not None:
        in_specs += [_resident((d, LANES)), _resident((1, LANES))]
        args += [wf, bf]
    kinds = {
        "q": (row_spec(d), (b, rows, d), BF16), "k": (row_spec(d), (b, rows, d), BF16),
        "v": (row_spec(d), (b, rows, d), BF16),
        "vt": (pl.BlockSpec((1, d, ts), lambda i, j: (i, 0, j)), (b, d, rows), BF16),
        "k32": (row_spec(d), (b, rows, d), F32), "v32": (row_spec(d), (b, rows, d), F32),
        "logf": (row_spec(LANES), (b, rows, LANES), F32),
    }
    stack_spec = lambda n: pl.BlockSpec((n, 1, d, ts), lambda i, j: (0, i, 0, j))
    kinds["k32t"] = kinds["v32t"] = (stack_spec(n_prev + 1), (n_prev + 1, b, d, rows), F32)
    if n_prev:
        in_specs += [stack_spec(n_prev)] * 2
        args += [state["k32t"], state["v32t"]]
    outs = pl.pallas_call(
        functools.partial(_qkv_body, d=d, emit=emit, n_prev=n_prev),
        grid=(b // bb, rows // ts),
        in_specs=in_specs,
        out_specs=[kinds[e][0] for e in emit],
        out_shape=[jax.ShapeDtypeStruct(kinds[e][1], kinds[e][2]) for e in emit],
        compiler_params=_params("parallel", "parallel"),
        name="norm_qkv",
    )(*args)
    return dict(zip(emit, outs))


def _post_attn_body(*refs, d_ff, final_norm):
    o_ref, x_ref, mod_ref, wo_ref, g_ref, wgu_ref, wd_ref = refs[:7]
    out_ref = refs[-1]
    x = x_ref[...]
    bb, ts, d = x.shape
    mod = lambda r: mod_ref[:, r:r + 1, :]
    y = jnp.dot(o_ref[...].reshape(bb * ts, d), wo_ref[...], preferred_element_type=F32)
    x1 = x + mod(2) * y.reshape(bb, ts, d)
    h = _modulated_norm(x1, g_ref[...], mod(4), mod(3))
    hb = h.reshape(bb * ts, d).astype(BF16)
    acc = jnp.zeros((bb * ts, d), F32)
    for c in range(d_ff // FF_CHUNK):
        lo = c * FF_CHUNK
        gate = jnp.dot(hb, wgu_ref[:, lo:lo + FF_CHUNK], preferred_element_type=F32)
        up = jnp.dot(hb, wgu_ref[:, d_ff + lo:d_ff + lo + FF_CHUNK], preferred_element_type=F32)
        act = (jax.nn.silu(gate) * up).astype(BF16)
        acc = acc + jnp.dot(act, wd_ref[lo:lo + FF_CHUNK, :], preferred_element_type=F32)
    x2 = x1 + mod(5) * acc.reshape(bb, ts, d)
    if final_norm:
        gout_ref = refs[7]
        ms = jnp.mean(x2 * x2, axis=-1, keepdims=True)
        x2 = x2 * lax.rsqrt(ms + RMS_EPS) * gout_ref[...]
    out_ref[...] = x2


def _post_attn(o, x, mod, w_o, g, w_gu, w_down, g_out=None, *, bb, ts):
    b, s, d = x.shape
    d_ff = w_down.shape[0]
    assert s % ts == 0 and b % bb == 0 and d_ff % FF_CHUNK == 0
    assert ts % 16 == 0 or bb == 1
    row = pl.BlockSpec((bb, ts, d), lambda i, j: (i, j, 0))
    in_specs = [row, row, pl.BlockSpec((bb, 6, d), lambda i, j: (i, 0, 0)), _resident((d, d)),
                _resident((1, d)), _resident((d, 2 * d_ff)), _resident((d_ff, d))]
    args = [o, x, mod, w_o, g.reshape(1, d), w_gu, w_down]
    if g_out is not None:
        in_specs.append(_resident((1, d)))
        args.append(g_out.reshape(1, d))
    return pl.pallas_call(
        functools.partial(_post_attn_body, d_ff=d_ff, final_norm=g_out is not None),
        grid=(b // bb, s // ts),
        in_specs=in_specs, out_specs=row,
        out_shape=jax.ShapeDtypeStruct((b, s, d), F32),
        compiler_params=_params("parallel", "parallel"),
        name="oproj_ffn",
    )(*args)


def _bias_row_body(tab_ref, g_ref):
    u = lax.broadcasted_iota(jnp.int32, (REL_PAD, G_WIDTH), 1)
    r = lax.broadcasted_iota(jnp.int32, (REL_PAD, G_WIDTH), 0)
    rel = A_PAST + (A_TILE - 1) - u
    idx = jnp.clip(rel, -(CHUNK - 1), MAX_REL) + (CHUNK - 1)
    onehot = jnp.where(r == idx, 1.0, 0.0).astype(BF16)
    hi, mid, lo = _split3(tab_ref[...])
    g_ref[...] = (jnp.dot(hi, onehot, preferred_element_type=F32)
                  + jnp.dot(mid, onehot, preferred_element_type=F32)
                  + jnp.dot(lo, onehot, preferred_element_type=F32))


def _bias_tile_body(g_ref, o_ref, ot_ref):
    x = jnp.broadcast_to(g_ref[0], (A_TILE, G_WIDTH))
    rolled = pltpu.roll(x, G_WIDTH - (A_TILE - 1), 1, stride=1, stride_axis=0)
    val = rolled[:, :A_WIN]
    qc = lax.broadcasted_iota(jnp.int32, (A_TILE, A_WIN), 0) // CHUNK
    kc = lax.broadcasted_iota(jnp.int32, (A_TILE, A_WIN), 1) // CHUNK
    band = (kc >= qc) & (kc <= qc + N_PAST_CHUNKS)
    tile = jnp.where(band, val, NEG_INF)
    o_ref[0] = tile
    ot_ref[0] = tile.T


def _rel_bias_tile(table):
    h, rel = table.shape
    assert rel == CHUNK + MAX_REL and rel <= REL_PAD and A_WIN + A_TILE - 1 <= G_WIDTH
    tab = jnp.pad(table, ((0, 0), (0, REL_PAD - rel)))
    gen = pl.pallas_call(
        _bias_row_body,
        out_shape=jax.ShapeDtypeStruct((h, G_WIDTH), F32),
        name="rel_bias_row",
    )(tab)
    return pl.pallas_call(
        _bias_tile_body,
        grid=(h,),
        in_specs=[pl.BlockSpec((1, 1, G_WIDTH), lambda i: (i, 0, 0))],
        out_specs=[pl.BlockSpec((1, A_TILE, A_WIN), lambda i: (i, 0, 0)),
                   pl.BlockSpec((1, A_WIN, A_TILE), lambda i: (i, 0, 0))],
        out_shape=[jax.ShapeDtypeStruct((h, A_TILE, A_WIN), F32),
                   jax.ShapeDtypeStruct((h, A_WIN, A_TILE), F32)],
        compiler_params=_params("parallel"),
        name="rel_bias_tile",
    )(gen.reshape(h, 1, G_WIDTH))


def _head_select(shape, hh):
    lower = _lower_lanes(shape)
    return lower if hh == 0 else jnp.logical_not(lower)


def _chunk_attn_body(q_ref, k_ref, vt_ref, bias_ref, o_ref, s0_ref, s1_ref, m0_ref, m1_ref,
                     *, n_tiles):
    sets = ((s0_ref, m0_ref), (s1_ref, m1_ref))
    lower = _lower_lanes((A_TILE, LANES))

    def head_queries(q):
        zero = jnp.zeros_like(q)
        return jnp.where(lower, q, zero), jnp.where(lower, zero, q)

    def head_values(vt):
        ones = jnp.ones((HEAD_DIM, vt.shape[1]), BF16)
        return (jnp.concatenate([vt[:HEAD_DIM], ones], axis=0),
                jnp.concatenate([ones, vt[HEAD_DIM:]], axis=0))

    def finish(accs):
        o0 = accs[0][:HEAD_DIM] / accs[0][HEAD_DIM:]
        o1 = accs[1][HEAD_DIM:] / accs[1][:HEAD_DIM]
        return jnp.concatenate([o0, o1], axis=0).T.astype(o_ref.dtype)

    def edge_tile(t):
        nk = (t + 1) * A_TILE
        qm = head_queries(q_ref[0, t * A_TILE:(t + 1) * A_TILE, :])
        kw = k_ref[0, 0:nk, :]
        vh = head_values(vt_ref[0, :, 0:nk])
        accs = []
        for hh in range(HEADS_PER_BLOCK):
            s = _nt_dot(kw, qm[hh]) + bias_ref[hh, A_WIN - nk:A_WIN, :]
            pt = jnp.exp(s - jnp.max(s, axis=0, keepdims=True)).astype(BF16)
            accs.append(jnp.dot(vh[hh], pt, preferred_element_type=F32))
        o_ref[0, t * A_TILE:(t + 1) * A_TILE, :] = finish(accs)

    def score_step(t, dst):
        s_ref, m_ref = sets[dst]
        qm = head_queries(q_ref[0, pl.ds(pl.multiple_of(t * A_TILE, A_TILE), A_TILE), :])
        kw = k_ref[0, pl.ds(pl.multiple_of((t - 2) * A_TILE, A_TILE), A_WIN), :]
        for hh in range(HEADS_PER_BLOCK):
            s = _nt_dot(kw, qm[hh]) + bias_ref[hh]
            s_ref[hh] = s
            m_ref[hh] = jnp.max(s, axis=0, keepdims=True)

    def value_step(t, src):
        s_ref, m_ref = sets[src]
        vh = head_values(vt_ref[0, :, pl.ds(pl.multiple_of((t - 2) * A_TILE, A_TILE), A_WIN)])
        accs = []
        for hh in range(HEADS_PER_BLOCK):
            pt = jnp.exp(s_ref[hh] - m_ref[hh]).astype(BF16)
            accs.append(jnp.dot(vh[hh], pt, preferred_element_type=F32))
        o_ref[0, pl.ds(pl.multiple_of(t * A_TILE, A_TILE), A_TILE), :] = finish(accs)

    edge_tile(0)
    edge_tile(1)
    score_step(2, 0)

    def two_tiles(u, carry):
        t = 2 + 2 * u
        score_step(t + 1, 1)
        value_step(t, 0)
        score_step(t + 2, 0)
        value_step(t + 1, 1)
        return carry

    lax.fori_loop(0, (n_tiles - 4) // 2, two_tiles, 0)
    score_step(n_tiles - 1, 1)
    value_step(n_tiles - 2, 0)
    value_step(n_tiles - 1, 1)


def _chunk_attn_prompt(q, k, vt, bias_t):
    b, s, d = q.shape
    n_tiles = s // A_TILE
    assert A_WIN == 3 * A_TILE and s % A_TILE == 0 and d % LANES == 0
    assert n_tiles >= 4 and n_tiles % 2 == 0
    pairs = d // LANES
    rows = pl.BlockSpec((1, s, LANES), lambda p, i: (i, 0, p))
    return pl.pallas_call(
        functools.partial(_chunk_attn_body, n_tiles=n_tiles),
        grid=(pairs, b),
        in_specs=[rows, rows, pl.BlockSpec((1, LANES, s), lambda p, i: (i, p, 0)),
                  pl.BlockSpec((HEADS_PER_BLOCK, A_WIN, A_TILE), lambda p, i: (p, 0, 0))],
        out_specs=rows,
        out_shape=jax.ShapeDtypeStruct((b, s, d), BF16),
        scratch_shapes=[pltpu.VMEM((HEADS_PER_BLOCK, A_WIN, A_TILE), F32)] * 2
        + [pltpu.VMEM((HEADS_PER_BLOCK, 1, A_TILE), F32)] * 2,
        compiler_params=_params("parallel", "parallel"),
        name="chunk_attn",
    )(q, k, vt, bias_t)


def _sample_attn_body(*refs, n_cache, forget):
    q_ref, kc_ref, vc_ref, kn_ref, vn_ref = refs[:5]
    o_ref = refs[-1]
    t, d = q_ref.shape[1:]
    if forget:
        fq_ref, fk_ref = refs[5:7]
        causal = (lax.broadcasted_iota(jnp.int32, (t, t), 1)
                  <= lax.broadcasted_iota(jnp.int32, (t, t), 0))
    else:
        bias_ref = refs[5]
    for p in range(d // LANES):
        lanes = slice(p * LANES, (p + 1) * LANES)
        q = q_ref[0, :, lanes]
        kc_t = kc_ref[0, 0, lanes, :].astype(BF16)
        vc_t = vc_ref[0, 0, lanes, :].astype(BF16)
        kn = kn_ref[0, :, lanes]
        vn = vn_ref[0, :, lanes]
        outs = []
        for hh in range(HEADS_PER_BLOCK):
            head = HEADS_PER_BLOCK * p + hh
            qm = jnp.where(_head_select(q.shape, hh), q, jnp.zeros_like(q))
            if forget:
                fq = fq_ref[0, :, head:head + 1]
                sc = (jnp.dot(qm, kc_t, preferred_element_type=F32)
                      + (fq - fk_ref[0, head:head + 1, 0:n_cache]))
                sn = _nt_dot(qm, kn) + (fq - fk_ref[0, head:head + 1, n_cache:n_cache + t])
                sn = jnp.where(causal, sn, NEG_INF)
            else:
                sc = (jnp.dot(qm, kc_t, preferred_element_type=F32)
                      + bias_ref[head, :, 0:n_cache])
                sn = _nt_dot(qm, kn) + bias_ref[head, :, n_cache:n_cache + t]
            m = jnp.maximum(jnp.max(sc, axis=-1, keepdims=True),
                            jnp.max(sn, axis=-1, keepdims=True))
            pc = jnp.exp(sc - m)
            pn = jnp.exp(sn - m)
            l = jnp.sum(pc, axis=-1, keepdims=True) + jnp.sum(pn, axis=-1, keepdims=True)
            o = (_nt_dot(pc.astype(BF16), vc_t)
                 + jnp.dot(pn.astype(BF16), vn, preferred_element_type=F32))
            outs.append(o / l)
        o_ref[0, :, lanes] = jnp.where(_lower_lanes(outs[0].shape), outs[0],
                                       outs[1]).astype(o_ref.dtype)


def _sample_attn(q, k, v, k_cache, v_cache, layer, *, bias=None, f_col_new=None, f_row=None):
    b, t, d = q.shape
    n_cache = k_cache.shape[3]
    new = pl.BlockSpec((1, t, d), lambda i: (i, 0, 0))
    cache = pl.BlockSpec((1, 1, d, n_cache), lambda i: (layer, i, 0, 0))
    forget = bias is None
    if forget:
        extra_specs = [pl.BlockSpec((1, t, LANES), lambda i: (i, 0, 0)),
                       pl.BlockSpec((1,) + f_row.shape[1:], lambda i: (i, 0, 0))]
        extra = [f_col_new, f_row]
    else:
        assert n_cache == A_PAST and t <= CHUNK and n_cache + t <= A_WIN
        extra_specs = [pl.BlockSpec((bias.shape[0], t, A_WIN), lambda i: (0, 0, 0))]
        extra = [bias]
    return pl.pallas_call(
        functools.partial(_sample_attn_body, n_cache=n_cache, forget=forget),
        grid=(b,),
        in_specs=[new, cache, cache, new, new] + extra_specs,
        out_specs=new,
        out_shape=jax.ShapeDtypeStruct((b, t, d), BF16),
        compiler_params=_params("parallel"),
        name="sample_attn",
    )(q, k_cache, v_cache, k, v, *extra)


def _bias_lanes(f, d):
    pieces = jnp.concatenate(_split3(f), axis=1)
    col = lax.broadcasted_iota(jnp.int32, (3 * LANES, d), 1)
    row = lax.broadcasted_iota(jnp.int32, (3 * LANES, d), 0)
    lane = col % LANES
    head = HEADS_PER_BLOCK * (col // LANES) + jnp.where(lane >= HEAD_DIM, 0, 1)
    slot = lane % HEAD_DIM
    hit = (row % LANES) == head

    def pick(first):
        sel = jnp.where(hit & (slot == first + row // LANES), 1.0, 0.0).astype(BF16)
        return jnp.dot(pieces, sel, preferred_element_type=F32)

    slot_row = lax.broadcasted_iota(jnp.int32, (1, d), 1) % HEAD_DIM
    ones = lambda first: jnp.where((slot_row >= first) & (slot_row < first + 3), 1.0, 0.0)
    q_side = pick(0) + ones(3)
    k_side = ones(0) - pick(3)
    return q_side.astype(BF16), k_side.astype(BF16)


def _cumsum_body(x_ref, *refs, tc, n_heads, d, emit_bias_lanes):
    carry_ref = refs[-1]

    @pl.when(pl.program_id(1) == 0)
    def _():
        carry_ref[...] = jnp.zeros_like(carry_ref)

    r = lax.broadcasted_iota(jnp.int32, (tc, tc), 0)
    c = lax.broadcasted_iota(jnp.int32, (tc, tc), 1)
    tri = jnp.where(c <= r, 1.0, 0.0).astype(BF16)
    pieces = jnp.concatenate(_split3(x_ref[0]), axis=1)
    sums = jnp.dot(tri, pieces, preferred_element_type=F32)
    f = ((sums[:, 0:LANES] + sums[:, LANES:2 * LANES]) + sums[:, 2 * LANES:3 * LANES]
         + carry_ref[0:1, :])
    carry_ref[0:1, :] = f[tc - 1:tc, :]
    if emit_bias_lanes:
        qa_ref, ka_ref = refs[:2]
        qa_ref[0], ka_ref[0] = _bias_lanes(f, d)
    else:
        fcol_ref, frow_ref = refs[:2]
        fcol_ref[0] = f
        frow_ref[0] = f.T[0:n_heads, :]


def _cumsum(logf, *, tc, d, emit_bias_lanes, n_heads=N_HEADS):
    b, s, w = logf.shape
    assert s % tc == 0 and tc % LANES == 0 and w == LANES
    if emit_bias_lanes:
        out_specs = [pl.BlockSpec((1, tc, d), lambda i, j: (i, j, 0))] * 2
        out_shape = [jax.ShapeDtypeStruct((b, s, d), BF16)] * 2
    else:
        out_specs = [pl.BlockSpec((1, tc, LANES), lambda i, j: (i, j, 0)),
                     pl.BlockSpec((1, n_heads, tc), lambda i, j: (i, 0, j))]
        out_shape = [jax.ShapeDtypeStruct((b, s, LANES), F32),
                     jax.ShapeDtypeStruct((b, n_heads, s), F32)]
    return pl.pallas_call(
        functools.partial(_cumsum_body, tc=tc, n_heads=n_heads, d=d,
                          emit_bias_lanes=emit_bias_lanes),
        grid=(b, s // tc),
        in_specs=[pl.BlockSpec((1, tc, LANES), lambda i, j: (i, j, 0))],
        out_specs=out_specs, out_shape=out_shape,
        scratch_shapes=[pltpu.VMEM((8, LANES), F32)],
        compiler_params=_params("parallel", "arbitrary"),
        name="logf_cumsum",
    )(logf)


def _fox_body(*refs, tile, n_blocks):
    def one_block(i, carry):
        _fox_query_block(i, *refs, tile=tile)
        return carry

    lax.fori_loop(0, n_blocks, one_block, 0)


def _fox_query_block(i, q_ref, qa_ref, k_ref, ka_ref, vt_ref, o_ref, qh_ref, acc_ref,
                     s0_ref, s1_ref, m0_ref, m1_ref, a0_ref, a1_ref, *, tile):
    bufs = ((s0_ref, m0_ref, a0_ref), (s1_ref, m1_ref, a1_ref))
    rows = pl.ds(pl.multiple_of(i * tile, tile), tile)
    qw = tile // FOX_QUERY_SPLIT
    chains = [(hh, qs) for hh in range(HEADS_PER_BLOCK) for qs in range(FOX_QUERY_SPLIT)]
    lower = _lower_lanes((tile, LANES))
    q = q_ref[0, rows, :]
    qa = qa_ref[0, rows, :]
    qh_ref[0] = jnp.where(lower, q, qa)
    qh_ref[1] = jnp.where(lower, qa, q)
    acc_ref[...] = jnp.zeros_like(acc_ref)
    ones = jnp.ones((HEAD_DIM, tile), BF16)

    def score_step(j, dst, src, diagonal=False):
        s_ref, m_ref, a_ref = bufs[dst]
        off = pl.multiple_of(j * tile, tile)
        kb = k_ref[0, pl.ds(off, tile), :]
        ka = ka_ref[0, pl.ds(off, tile), :]
        kh = (jnp.where(lower, kb, ka), jnp.where(lower, ka, kb))
        for c, (hh, qs) in enumerate(chains):
            s = _nt_dot(kh[hh], qh_ref[hh, qs * qw:(qs + 1) * qw, :])
            if diagonal:
                key = lax.broadcasted_iota(jnp.int32, s.shape, 0)
                qry = lax.broadcasted_iota(jnp.int32, s.shape, 1) + qs * qw
                s = jnp.where(key <= qry, s, NEG_INF)
            m_prev = bufs[src][1][c]
            m_new = jnp.maximum(m_prev, jnp.max(s, axis=0, keepdims=True))
            s_ref[c] = s
            m_ref[c] = m_new
            a_ref[c] = jnp.exp(m_prev - m_new)

    def value_step(j, src):
        s_ref, m_ref, a_ref = bufs[src]
        off = pl.multiple_of(j * tile, tile)
        vt = vt_ref[0, :, pl.ds(off, tile)]
        vh = (jnp.concatenate([vt[:HEAD_DIM], ones], axis=0),
              jnp.concatenate([ones, vt[HEAD_DIM:]], axis=0))
        for c, (hh, qs) in enumerate(chains):
            pt = jnp.exp(s_ref[c] - m_ref[c]).astype(BF16)
            cols = slice(qs * qw, (qs + 1) * qw)
            acc_ref[hh, :, cols] = (a_ref[c] * acc_ref[hh, :, cols]
                                    + jnp.dot(vh[hh], pt, preferred_element_type=F32))

    block_of_step = lambda t: jnp.where(t == 0, i, t - 1)
    m1_ref[...] = jnp.full(m1_ref.shape, NEG_INF, F32)
    score_step(i, 0, 1, diagonal=True)

    def two_steps(pp):
        t = 2 * pp
        score_step(t, 1, 0)
        value_step(block_of_step(t), 0)
        score_step(t + 1, 0, 1)
        value_step(t, 1)

    def four_steps(u, carry):
        two_steps(2 * u)
        two_steps(2 * u + 1)
        return carry

    pairs_of_steps = i // 2
    lax.fori_loop(0, pairs_of_steps // 2, four_steps, 0)

    @pl.when(pairs_of_steps % 2 == 1)
    def _():
        two_steps(pairs_of_steps - 1)

    @pl.when(i % 2 == 1)
    def _():
        score_step(i - 1, 1, 0)
        value_step(block_of_step(i - 1), 0)
        value_step(i - 1, 1)

    @pl.when(i % 2 == 0)
    def _():
        value_step(block_of_step(i), 0)

    o0 = acc_ref[0, :HEAD_DIM, :] / acc_ref[0, HEAD_DIM:, :]
    o1 = acc_ref[1, HEAD_DIM:, :] / acc_ref[1, :HEAD_DIM, :]
    o_ref[0, rows, :] = jnp.concatenate([o0, o1], axis=0).T.astype(o_ref.dtype)


def _fox_attn_prompt(q, qa, k, ka, vt):
    b, s, d = q.shape
    tile = FOX_TILE
    assert s % tile == 0
    pairs = d // LANES
    n_chains = HEADS_PER_BLOCK * FOX_QUERY_SPLIT
    qw = tile // FOX_QUERY_SPLIT
    seq = pl.BlockSpec((1, s, LANES), lambda i, p: (i, 0, p))
    return pl.pallas_call(
        functools.partial(_fox_body, tile=tile, n_blocks=s // tile),
        grid=(b, pairs),
        in_specs=[seq, seq, seq, seq, pl.BlockSpec((1, LANES, s), lambda i, p: (i, p, 0))],
        out_specs=seq,
        out_shape=jax.ShapeDtypeStruct((b, s, d), BF16),
        scratch_shapes=[pltpu.VMEM((HEADS_PER_BLOCK, tile, LANES), BF16),
                        pltpu.VMEM((HEADS_PER_BLOCK, LANES, tile), F32)]
        + [pltpu.VMEM((n_chains, tile, qw), F32)] * 2
        + [pltpu.VMEM((n_chains, 1, qw), F32)] * 4,
        compiler_params=_params("parallel", "parallel"),
        name="fox_attn",
    )(q, qa, k, ka, vt)


def _trunk(x, mods, weights, a_mix, b_mix, *, bb, ts, keep, v_name):
    b, s, d = x.shape
    depth = mods.shape[0]
    transposed = v_name == "vt"
    k_name, v32_name = ("k32t", "v32t") if transposed else ("k32", "v32")
    a_state, b_state, b_logf = None, None, []
    stacked = lambda st: {k_name: jnp.stack([e[k_name] for e in st]),
                          v32_name: jnp.stack([e[v32_name] for e in st])}
    a_list, b_list = [], []
    for l in range(depth):
        mod = mods[l]
        g, w = weights["g_mix"][l], weights["w_qkv"][l]
        if l % 2 == 0:
            kw = dict(state=a_state) if transposed else {}
            if keep == s:
                proj = kv32 = _qkv_proj(x, mod, g, w, bb=bb, ts=ts,
                                        emit=("q", "k", v_name, k_name, v32_name), **kw)
            else:
                proj = _qkv_proj(x, mod, g, w, bb=bb, ts=ts, emit=("q", "k", v_name))
                kv32 = _qkv_proj(x, mod, g, w, bb=bb, ts=min(ts, keep), emit=(k_name, v32_name),
                                 row_start=s - keep, rows=keep, **kw)
            a_state = kv32
            a_list.append(kv32)
            o = a_mix(l // 2, proj)
        else:
            ib = l // 2
            kw = dict(state=b_state) if transposed else {}
            ts_b = ts // 2 if transposed and b_state is not None else ts
            proj = _qkv_proj(x, mod, g, w, weights["w_fgate"][ib], weights["b_fgate"][ib],
                             bb=bb, ts=ts_b, emit=("q", "k", v_name, k_name, v32_name, "logf"),
                             **kw)
            b_state = proj
            b_list.append(proj)
            b_logf.append(proj["logf"][:, :, :N_HEADS])
            o = b_mix(ib, proj)
        g_out = weights["g_out"] if l == depth - 1 else None
        x = _post_attn(o, x, mod, weights["w_o"][l], weights["g_ffn"][l], weights["w_gu"][l],
                       weights["w_down"][l], g_out, bb=bb, ts=ts)
    if not transposed:
        a_state, b_state = stacked(a_list), stacked(b_list)

    def heads(a):
        if transposed:
            n, bsz, _, r = a.shape
            return a.reshape(n, bsz, N_HEADS, HEAD_DIM, r).transpose(0, 1, 4, 2, 3)
        return a.reshape(a.shape[:3] + (N_HEADS, HEAD_DIM))

    return (x, heads(a_state[k_name]), heads(a_state[v32_name]),
            heads(b_state[k_name]), heads(b_state[v32_name]), jnp.stack(b_logf))


def kernel(x_prompt, x_sample, cache_a_k, cache_a_v, cache_b_k, cache_b_v, cache_b_logf,
           c_prompt, c_sample, w_mod, b_mod, g_mix, g_ffn, w_qkv, w_o, rel_bias,
           w_fgate, b_fgate, w_gu, w_down, g_out):
    bp, sp, d = x_prompt.shape
    bs, t_new, _ = x_sample.shape
    depth = w_mod.shape[0]
    n_b = w_fgate.shape[0]
    assert d == N_HEADS * HEAD_DIM
    keep = min(A_PAST, sp)

    weights = {
        "g_mix": g_mix, "g_ffn": g_ffn, "g_out": g_out,
        "w_qkv": w_qkv.astype(BF16), "w_o": w_o.astype(BF16),
        "w_gu": w_gu.astype(BF16), "w_down": w_down.astype(BF16),
        "w_fgate": jnp.pad(w_fgate, ((0, 0), (0, 0), (0, LANES - N_HEADS))).astype(BF16),
        "b_fgate": jnp.pad(b_fgate, ((0, 0), (0, LANES - N_HEADS))).reshape(n_b, 1, LANES),
    }

    c_all = jnp.concatenate([c_prompt, c_sample], axis=0)
    rows = -(-c_all.shape[0] // 16) * 16
    c_all = jnp.pad(c_all, ((0, rows - c_all.shape[0]), (0, 0)))
    mods = _modulation(c_all, w_mod, b_mod)
    mods_p = mods[:, :bp].reshape(depth, bp, 6, d)
    mods_s = mods[:, bp:bp + bs].reshape(depth, bs, 6, d)

    bias = [_rel_bias_tile(rel_bias[i]) for i in range(rel_bias.shape[0])]

    def a_prompt(i, proj):
        return _chunk_attn_prompt(proj["q"], proj["k"], proj["vt"], bias[i][1])

    def b_prompt(i, proj):
        qa, ka = _cumsum(proj["logf"], tc=FOX_TILE, d=d, emit_bias_lanes=True)
        return _fox_attn_prompt(proj["q"], qa, proj["k"], ka, proj["vt"])

    prompt_out = _trunk(x_prompt, mods_p, weights, a_prompt, b_prompt, bb=1, ts=ROW_TILE,
                              keep=keep, v_name="vt")

    by_feature = lambda c: c.transpose(0, 1, 3, 4, 2).reshape(c.shape[0], bs, d, c.shape[2])
    ca_k, ca_v = by_feature(cache_a_k), by_feature(cache_a_v)
    cb_k, cb_v = by_feature(cache_b_k), by_feature(cache_b_v)
    past = cache_b_k.shape[2]
    total = -(-(past + t_new) // SAMPLE_CUMSUM_TILE) * SAMPLE_CUMSUM_TILE

    def a_sample(i, proj):
        return _sample_attn(proj["q"], proj["k"], proj["v"], ca_k, ca_v, i, bias=bias[i][0])

    def b_sample(i, proj):
        lf_cache = jnp.pad(cache_b_logf[i].astype(F32), ((0, 0), (0, 0), (0, LANES - N_HEADS)))
        lf = jnp.concatenate([lf_cache, proj["logf"]], axis=1)
        lf = jnp.pad(lf, ((0, 0), (0, total - (past + t_new)), (0, 0)))
        f_col, f_row = _cumsum(lf, tc=SAMPLE_CUMSUM_TILE, d=d, emit_bias_lanes=False)
        return _sample_attn(proj["q"], proj["k"], proj["v"], cb_k, cb_v, i,
                            f_col_new=f_col[:, past:past + t_new], f_row=f_row)

    sample_out = _trunk(x_sample, mods_s, weights, a_sample, b_sample, bb=bs, ts=t_new,
                              keep=t_new, v_name="v")

    return (prompt_out[0], sample_out[0]) + tuple(prompt_out[1:]) + tuple(sample_out[1:])
```

```python
import functools

import jax
import jax.numpy as jnp
from jax import lax
from jax.experimental import pallas as pl
from jax.experimental.pallas import tpu as pltpu

F32 = jnp.float32
BF16 = jnp.bfloat16

N_HEADS = 16
HEAD_DIM = 64
CHUNK = 64
N_PAST_CHUNKS = 8
A_PAST = N_PAST_CHUNKS * CHUNK
MAX_REL = 256
RMS_EPS = 1e-6
NEG_INF = -1e30
ATTN_SCALE = HEAD_DIM ** -0.5

LANES = 128
HEADS_PER_BLOCK = LANES // HEAD_DIM
VMEM_LIMIT_BYTES = 48 * 1024 * 1024

ROW_TILE = 512
A_TILE = 256
A_WIN = A_TILE + A_PAST
FOX_TILE = 512
FOX_QUERY_SPLIT = 2
FF_CHUNK = 256
REL_PAD = 384
G_WIDTH = 1024
SAMPLE_CUMSUM_TILE = 384


def _params(*sem):
    return pltpu.CompilerParams(dimension_semantics=sem, vmem_limit_bytes=VMEM_LIMIT_BYTES)


def _resident(shape):
    zeros = (0,) * len(shape)
    return pl.BlockSpec(shape, lambda *_: zeros, pipeline_mode=pl.Buffered(1))


def _nt_dot(a, b):
    return lax.dot_general(a, b, (((1,), (1,)), ((), ())), preferred_element_type=F32)


def _modulated_norm(x, g, scale, shift):
    ms = jnp.mean(x * x, axis=-1, keepdims=True)
    y = x * lax.rsqrt(ms + RMS_EPS) * g
    return y * (1.0 + scale) + shift


def _split3(x):
    hi = x.astype(BF16)
    r1 = x - hi.astype(F32)
    mid = r1.astype(BF16)
    lo = (r1 - mid.astype(F32)).astype(BF16)
    return hi, mid, lo


def _lower_lanes(shape):
    return lax.broadcasted_iota(jnp.int32, shape, len(shape) - 1) < HEAD_DIM


def _mod_body(c_ref, w_ref, b_ref, o_ref):
    cs = jax.nn.silu(c_ref[...]).astype(BF16)
    o_ref[0] = jnp.dot(cs, w_ref[0].astype(BF16), preferred_element_type=F32) + b_ref[0]


def _modulation(c, w_mod, b_mod):
    depth, d, n = w_mod.shape
    rows = c.shape[0]
    tn = 1536
    assert n % tn == 0 and rows % 16 == 0
    return pl.pallas_call(
        _mod_body,
        grid=(depth, n // tn),
        in_specs=[pl.BlockSpec((rows, d), lambda l, j: (0, 0)),
                  pl.BlockSpec((1, d, tn), lambda l, j: (l, 0, j)),
                  pl.BlockSpec((1, 1, tn), lambda l, j: (l, 0, j))],
        out_specs=pl.BlockSpec((1, rows, tn), lambda l, j: (l, 0, j)),
        out_shape=jax.ShapeDtypeStruct((depth, rows, n), F32),
        compiler_params=_params("parallel", "parallel"),
        name="modulation",
    )(c, w_mod, b_mod.reshape(depth, 1, n))


def _qkv_body(*refs, d, emit, n_prev):
    x_ref, mod_ref, g_ref, w_ref = refs[:4]
    pos = 4
    if "logf" in emit:
        wf_ref, bf_ref = refs[4:6]
        pos = 6
    prev = dict(zip(("k32t", "v32t"), refs[pos:pos + (2 if n_prev else 0)]))
    out = dict(zip(emit, refs[pos + len(prev):]))
    for name, prev_ref in prev.items():
        out[name][0:n_prev] = prev_ref[...]
    x = x_ref[...]
    bb, ts, _ = x.shape
    h = _modulated_norm(x, g_ref[...], mod_ref[:, 1:2, :], mod_ref[:, 0:1, :])
    hb = h.reshape(bb * ts, d).astype(BF16)

    def put(name, val, dtype):
        out[name][...] = val.reshape(bb, ts, val.shape[-1]).astype(dtype)

    if "q" in emit:
        q = jnp.dot(hb, w_ref[:, 0:d], preferred_element_type=F32)
        put("q", q * ATTN_SCALE, BF16)
    k = jnp.dot(hb, w_ref[:, d:2 * d], preferred_element_type=F32)
    v = jnp.dot(hb, w_ref[:, 2 * d:3 * d], preferred_element_type=F32)
    if "k" in emit:
        put("k", k, BF16)
    if "v" in emit:
        put("v", v, BF16)
    if "vt" in emit or "v32t" in emit:
        v_t = v.T
    if "vt" in emit:
        out["vt"][0] = v_t.astype(BF16)
    if "v32t" in emit:
        out["v32t"][n_prev, 0] = v_t
    if "k32t" in emit:
        out["k32t"][n_prev, 0] = k.T
    if "k32" in emit:
        put("k32", k, F32)
    if "v32" in emit:
        put("v32", v, F32)
    if "logf" in emit:
        z = jnp.dot(hb, wf_ref[...], preferred_element_type=F32) + bf_ref[...]
        put("logf", jax.nn.log_sigmoid(z), F32)


def _qkv_proj(x, mod, g, w_qkv, wf=None, bf=None, *, bb, ts, emit, row_start=0, rows=None,
              state=None):
    b, s, d = x.shape
    rows = s if rows is None else rows
    assert rows % ts == 0 and row_start % ts == 0 and b % bb == 0
    assert ts % 16 == 0 or bb == 1
    assert bb == 1 or not {"vt", "k32t", "v32t"} & set(emit)
    off = row_start // ts
    n_prev = 0 if state is None else state["k32t"].shape[0]
    assert ("logf" in emit) == (wf is not None)
    row_spec = lambda width: pl.BlockSpec((bb, ts, width), lambda i, j: (i, j, 0))
    in_specs = [pl.BlockSpec((bb, ts, d), lambda i, j: (i, j + off, 0)),
                pl.BlockSpec((bb, 6, d), lambda i, j: (i, 0, 0)),
                _resident((1, d)),
                _resident((d, 3 * d))]
    args = [x, mod, g.reshape(1, d), w_qkv]
    if wf is not None:
        in_specs += [_resident((d, LANES)), _resident((1, LANES))]
        args += [wf, bf]
    kinds = {
        "q": (row_spec(d), (b, rows, d), BF16), "k": (row_spec(d), (b, rows, d), BF16),
        "v": (row_spec(d), (b, rows, d), BF16),
        "vt": (pl.BlockSpec((1, d, ts), lambda i, j: (i, 0, j)), (b, d, rows), BF16),
        "k32": (row_spec(d), (b, rows, d), F32), "v32": (row_spec(d), (b, rows, d), F32),
        "logf": (row_spec(LANES), (b, rows, LANES), F32),
    }
    stack_spec = lambda n: pl.BlockSpec((n, 1, d, ts), lambda i, j: (0, i, 0, j))
    kinds["k32t"] = kinds["v32t"] = (stack_spec(n_prev + 1), (n_prev + 1, b, d, rows), F32)
    if n_prev:
        in_specs += [stack_spec(n_prev)] * 2
        args += [state["k32t"], state["v32t"]]
    outs = pl.pallas_call(
        functools.partial(_qkv_body, d=d, emit=emit, n_prev=n_prev),
        grid=(b // bb, rows // ts),
        in_specs=in_specs,
        out_specs=[kinds[e][0] for e in emit],
        out_shape=[jax.ShapeDtypeStruct(kinds[e][1], kinds[e][2]) for e in emit],
        compiler_params=_params("parallel", "parallel"),
        name="norm_qkv",
    )(*args)
    return dict(zip(emit, outs))


def _post_attn_body(*refs, d_ff, final_norm):
    o_ref, x_ref, mod_ref, wo_ref, g_ref, wgu_ref, wd_ref = refs[:7]
    out_ref = refs[-1]
    x = x_ref[...]
    bb, ts, d = x.shape
    mod = lambda r: mod_ref[:, r:r + 1, :]
    y = jnp.dot(o_ref[...].reshape(bb * ts, d), wo_ref[...], preferred_element_type=F32)
    x1 = x + mod(2) * y.reshape(bb, ts, d)
    h = _modulated_norm(x1, g_ref[...], mod(4), mod(3))
    hb = h.reshape(bb * ts, d).astype(BF16)
    acc = jnp.zeros((bb * ts, d), F32)
    for c in range(d_ff // FF_CHUNK):
        lo = c * FF_CHUNK
        gate = jnp.dot(hb, wgu_ref[:, lo:lo + FF_CHUNK], preferred_element_type=F32)
        up = jnp.dot(hb, wgu_ref[:, d_ff + lo:d_ff + lo + FF_CHUNK], preferred_element_type=F32)
        act = (jax.nn.silu(gate) * up).astype(BF16)
        acc = acc + jnp.dot(act, wd_ref[lo:lo + FF_CHUNK, :], preferred_element_type=F32)
    x2 = x1 + mod(5) * acc.reshape(bb, ts, d)
    if final_norm:
        gout_ref = refs[7]
        ms = jnp.mean(x2 * x2, axis=-1, keepdims=True)
        x2 = x2 * lax.rsqrt(ms + RMS_EPS) * gout_ref[...]
    out_ref[...] = x2


def _post_attn(o, x, mod, w_o, g, w_gu, w_down, g_out=None, *, bb, ts):
    b, s, d = x.shape
    d_ff = w_down.shape[0]
    assert s % ts == 0 and b % bb == 0 and d_ff % FF_CHUNK == 0
    assert ts % 16 == 0 or bb == 1
    row = pl.BlockSpec((bb, ts, d), lambda i, j: (i, j, 0))
    in_specs = [row, row, pl.BlockSpec((bb, 6, d), lambda i, j: (i, 0, 0)), _resident((d, d)),
                _resident((1, d)), _resident((d, 2 * d_ff)), _resident((d_ff, d))]
    args = [o, x, mod, w_o, g.reshape(1, d), w_gu, w_down]
    if g_out is not None:
        in_specs.append(_resident((1, d)))
        args.append(g_out.reshape(1, d))
    return pl.pallas_call(
        functools.partial(_post_attn_body, d_ff=d_ff, final_norm=g_out is not None),
        grid=(b // bb, s // ts),
        in_specs=in_specs, out_specs=row,
        out_shape=jax.ShapeDtypeStruct((b, s, d), F32),
        compiler_params=_params("parallel", "parallel"),
        name="oproj_ffn",
    )(*args)


def _bias_row_body(tab_ref, g_ref):
    u = lax.broadcasted_iota(jnp.int32, (REL_PAD, G_WIDTH), 1)
    r = lax.broadcasted_iota(jnp.int32, (REL_PAD, G_WIDTH), 0)
    rel = A_PAST + (A_TILE - 1) - u
    idx = jnp.clip(rel, -(CHUNK - 1), MAX_REL) + (CHUNK - 1)
    onehot = jnp.where(r == idx, 1.0, 0.0).astype(BF16)
    hi, mid, lo = _split3(tab_ref[...])
    g_ref[...] = (jnp.dot(hi, onehot, preferred_element_type=F32)
                  + jnp.dot(mid, onehot, preferred_element_type=F32)
                  + jnp.dot(lo, onehot, preferred_element_type=F32))


def _bias_tile_body(g_ref, o_ref, ot_ref):
    x = jnp.broadcast_to(g_ref[0], (A_TILE, G_WIDTH))
    rolled = pltpu.roll(x, G_WIDTH - (A_TILE - 1), 1, stride=1, stride_axis=0)
    val = rolled[:, :A_WIN]
    qc = lax.broadcasted_iota(jnp.int32, (A_TILE, A_WIN), 0) // CHUNK
    kc = lax.broadcasted_iota(jnp.int32, (A_TILE, A_WIN), 1) // CHUNK
    band = (kc >= qc) & (kc <= qc + N_PAST_CHUNKS)
    tile = jnp.where(band, val, NEG_INF)
    o_ref[0] = tile
    ot_ref[0] = tile.T


def _rel_bias_tile(table):
    h, rel = table.shape
    assert rel == CHUNK + MAX_REL and rel <= REL_PAD and A_WIN + A_TILE - 1 <= G_WIDTH
    tab = jnp.pad(table, ((0, 0), (0, REL_PAD - rel)))
    gen = pl.pallas_call(
        _bias_row_body,
        out_shape=jax.ShapeDtypeStruct((h, G_WIDTH), F32),
        name="rel_bias_row",
    )(tab)
    return pl.pallas_call(
        _bias_tile_body,
        grid=(h,),
        in_specs=[pl.BlockSpec((1, 1, G_WIDTH), lambda i: (i, 0, 0))],
        out_specs=[pl.BlockSpec((1, A_TILE, A_WIN), lambda i: (i, 0, 0)),
                   pl.BlockSpec((1, A_WIN, A_TILE), lambda i: (i, 0, 0))],
        out_shape=[jax.ShapeDtypeStruct((h, A_TILE, A_WIN), F32),
                   jax.ShapeDtypeStruct((h, A_WIN, A_TILE), F32)],
        compiler_params=_params("parallel"),
        name="rel_bias_tile",
    )(gen.reshape(h, 1, G_WIDTH))


def _head_select(shape, hh):
    lower = _lower_lanes(shape)
    return lower if hh == 0 else jnp.logical_not(lower)


def _chunk_attn_body(q_ref, k_ref, vt_ref, bias_ref, o_ref, s0_ref, s1_ref, m0_ref, m1_ref,
                     *, n_tiles):
    sets = ((s0_ref, m0_ref), (s1_ref, m1_ref))
    lower = _lower_lanes((A_TILE, LANES))

    def head_queries(q):
        zero = jnp.zeros_like(q)
        return jnp.where(lower, q, zero), jnp.where(lower, zero, q)

    def head_values(vt):
        ones = jnp.ones((HEAD_DIM, vt.shape[1]), BF16)
        return (jnp.concatenate([vt[:HEAD_DIM], ones], axis=0),
                jnp.concatenate([ones, vt[HEAD_DIM:]], axis=0))

    def finish(accs):
        o0 = accs[0][:HEAD_DIM] / accs[0][HEAD_DIM:]
        o1 = accs[1][HEAD_DIM:] / accs[1][:HEAD_DIM]
        return jnp.concatenate([o0, o1], axis=0).T.astype(o_ref.dtype)

    def edge_tile(t):
        nk = (t + 1) * A_TILE
        qm = head_queries(q_ref[0, t * A_TILE:(t + 1) * A_TILE, :])
        kw = k_ref[0, 0:nk, :]
        vh = head_values(vt_ref[0, :, 0:nk])
        accs = []
        for hh in range(HEADS_PER_BLOCK):
            s = _nt_dot(kw, qm[hh]) + bias_ref[hh, A_WIN - nk:A_WIN, :]
            pt = jnp.exp(s - jnp.max(s, axis=0, keepdims=True)).astype(BF16)
            accs.append(jnp.dot(vh[hh], pt, preferred_element_type=F32))
        o_ref[0, t * A_TILE:(t + 1) * A_TILE, :] = finish(accs)

    def score_step(t, dst):
        s_ref, m_ref = sets[dst]
        qm = head_queries(q_ref[0, pl.ds(pl.multiple_of(t * A_TILE, A_TILE), A_TILE), :])
        kw = k_ref[0, pl.ds(pl.multiple_of((t - 2) * A_TILE, A_TILE), A_WIN), :]
        for hh in range(HEADS_PER_BLOCK):
            s = _nt_dot(kw, qm[hh]) + bias_ref[hh]
            s_ref[hh] = s
            m_ref[hh] = jnp.max(s, axis=0, keepdims=True)

    def value_step(t, src):
        s_ref, m_ref = sets[src]
        vh = head_values(vt_ref[0, :, pl.ds(pl.multiple_of((t - 2) * A_TILE, A_TILE), A_WIN)])
        accs = []
        for hh in range(HEADS_PER_BLOCK):
            pt = jnp.exp(s_ref[hh] - m_ref[hh]).astype(BF16)
            accs.append(jnp.dot(vh[hh], pt, preferred_element_type=F32))
        o_ref[0, pl.ds(pl.multiple_of(t * A_TILE, A_TILE), A_TILE), :] = finish(accs)

    edge_tile(0)
    edge_tile(1)
    score_step(2, 0)

    def two_tiles(u):
        t = 2 + 2 * u
        score_step(t + 1, 1)
        value_step(t, 0)
        score_step(t + 2, 0)
        value_step(t + 1, 1)

    def four_tiles(w, carry):
        two_tiles(2 * w)
        two_tiles(2 * w + 1)
        return carry

    pairs_of_tiles = (n_tiles - 4) // 2
    lax.fori_loop(0, pairs_of_tiles // 2, four_tiles, 0)
    if pairs_of_tiles % 2:
        two_tiles(pairs_of_tiles - 1)
    score_step(n_tiles - 1, 1)
    value_step(n_tiles - 2, 0)
    value_step(n_tiles - 1, 1)


def _chunk_attn_prompt(q, k, vt, bias_t):
    b, s, d = q.shape
    n_tiles = s // A_TILE
    assert A_WIN == 3 * A_TILE and s % A_TILE == 0 and d % LANES == 0
    assert n_tiles >= 4 and n_tiles % 2 == 0
    pairs = d // LANES
    rows = pl.BlockSpec((1, s, LANES), lambda p, i: (i, 0, p))
    return pl.pallas_call(
        functools.partial(_chunk_attn_body, n_tiles=n_tiles),
        grid=(pairs, b),
        in_specs=[rows, rows, pl.BlockSpec((1, LANES, s), lambda p, i: (i, p, 0)),
                  pl.BlockSpec((HEADS_PER_BLOCK, A_WIN, A_TILE), lambda p, i: (p, 0, 0))],
        out_specs=rows,
        out_shape=jax.ShapeDtypeStruct((b, s, d), BF16),
        scratch_shapes=[pltpu.VMEM((HEADS_PER_BLOCK, A_WIN, A_TILE), F32)] * 2
        + [pltpu.VMEM((HEADS_PER_BLOCK, 1, A_TILE), F32)] * 2,
        compiler_params=_params("parallel", "parallel"),
        name="chunk_attn",
    )(q, k, vt, bias_t)


def _sample_attn_body(*refs, n_cache, forget):
    q_ref, kc_ref, vc_ref, kn_ref, vn_ref = refs[:5]
    o_ref = refs[-1]
    t, d = q_ref.shape[1:]
    if forget:
        fq_ref, fk_ref = refs[5:7]
        causal = (lax.broadcasted_iota(jnp.int32, (t, t), 1)
                  <= lax.broadcasted_iota(jnp.int32, (t, t), 0))
    else:
        bias_ref = refs[5]
    for p in range(d // LANES):
        lanes = slice(p * LANES, (p + 1) * LANES)
        q = q_ref[0, :, lanes]
        kc_t = kc_ref[0, 0, lanes, :].astype(BF16)
        vc_t = vc_ref[0, 0, lanes, :].astype(BF16)
        kn = kn_ref[0, :, lanes]
        vn = vn_ref[0, :, lanes]
        outs = []
        for hh in range(HEADS_PER_BLOCK):
            head = HEADS_PER_BLOCK * p + hh
            qm = jnp.where(_head_select(q.shape, hh), q, jnp.zeros_like(q))
            if forget:
                fq = fq_ref[0, :, head:head + 1]
                sc = (jnp.dot(qm, kc_t, preferred_element_type=F32)
                      + (fq - fk_ref[0, head:head + 1, 0:n_cache]))
                sn = _nt_dot(qm, kn) + (fq - fk_ref[0, head:head + 1, n_cache:n_cache + t])
                sn = jnp.where(causal, sn, NEG_INF)
            else:
                sc = (jnp.dot(qm, kc_t, preferred_element_type=F32)
                      + bias_ref[head, :, 0:n_cache])
                sn = _nt_dot(qm, kn) + bias_ref[head, :, n_cache:n_cache + t]
            m = jnp.maximum(jnp.max(sc, axis=-1, keepdims=True),
                            jnp.max(sn, axis=-1, keepdims=True))
            pc = jnp.exp(sc - m)
            pn = jnp.exp(sn - m)
            l = jnp.sum(pc, axis=-1, keepdims=True) + jnp.sum(pn, axis=-1, keepdims=True)
            o = (_nt_dot(pc.astype(BF16), vc_t)
                 + jnp.dot(pn.astype(BF16), vn, preferred_element_type=F32))
            outs.append(o / l)
        o_ref[0, :, lanes] = jnp.where(_lower_lanes(outs[0].shape), outs[0],
                                       outs[1]).astype(o_ref.dtype)


def _sample_attn(q, k, v, k_cache, v_cache, layer, *, bias=None, f_col_new=None, f_row=None):
    b, t, d = q.shape
    n_cache = k_cache.shape[3]
    new = pl.BlockSpec((1, t, d), lambda i: (i, 0, 0))
    cache = pl.BlockSpec((1, 1, d, n_cache), lambda i: (layer, i, 0, 0))
    forget = bias is None
    if forget:
        extra_specs = [pl.BlockSpec((1, t, LANES), lambda i: (i, 0, 0)),
                       pl.BlockSpec((1,) + f_row.shape[1:], lambda i: (i, 0, 0))]
        extra = [f_col_new, f_row]
    else:
        assert n_cache == A_PAST and t <= CHUNK and n_cache + t <= A_WIN
        extra_specs = [pl.BlockSpec((bias.shape[0], t, A_WIN), lambda i: (0, 0, 0))]
        extra = [bias]
    return pl.pallas_call(
        functools.partial(_sample_attn_body, n_cache=n_cache, forget=forget),
        grid=(b,),
        in_specs=[new, cache, cache, new, new] + extra_specs,
        out_specs=new,
        out_shape=jax.ShapeDtypeStruct((b, t, d), BF16),
        compiler_params=_params("parallel"),
        name="sample_attn",
    )(q, k_cache, v_cache, k, v, *extra)


def _bias_lanes(f, d):
    pieces = jnp.concatenate(_split3(f), axis=1)
    col = lax.broadcasted_iota(jnp.int32, (3 * LANES, d), 1)
    row = lax.broadcasted_iota(jnp.int32, (3 * LANES, d), 0)
    lane = col % LANES
    head = HEADS_PER_BLOCK * (col // LANES) + jnp.where(lane >= HEAD_DIM, 0, 1)
    slot = lane % HEAD_DIM
    hit = (row % LANES) == head

    def pick(first):
        sel = jnp.where(hit & (slot == first + row // LANES), 1.0, 0.0).astype(BF16)
        return jnp.dot(pieces, sel, preferred_element_type=F32)

    slot_row = lax.broadcasted_iota(jnp.int32, (1, d), 1) % HEAD_DIM
    ones = lambda first: jnp.where((slot_row >= first) & (slot_row < first + 3), 1.0, 0.0)
    q_side = pick(0) + ones(3)
    k_side = ones(0) - pick(3)
    return q_side.astype(BF16), k_side.astype(BF16)


def _cumsum_body(x_ref, *refs, tc, n_heads, d, emit_bias_lanes):
    carry_ref = refs[-1]

    @pl.when(pl.program_id(1) == 0)
    def _():
        carry_ref[...] = jnp.zeros_like(carry_ref)

    r = lax.broadcasted_iota(jnp.int32, (tc, tc), 0)
    c = lax.broadcasted_iota(jnp.int32, (tc, tc), 1)
    tri = jnp.where(c <= r, 1.0, 0.0).astype(BF16)
    pieces = jnp.concatenate(_split3(x_ref[0]), axis=1)
    sums = jnp.dot(tri, pieces, preferred_element_type=F32)
    f = ((sums[:, 0:LANES] + sums[:, LANES:2 * LANES]) + sums[:, 2 * LANES:3 * LANES]
         + carry_ref[0:1, :])
    carry_ref[0:1, :] = f[tc - 1:tc, :]
    if emit_bias_lanes:
        qa_ref, ka_ref = refs[:2]
        qa_ref[0], ka_ref[0] = _bias_lanes(f, d)
    else:
        fcol_ref, frow_ref = refs[:2]
        fcol_ref[0] = f
        frow_ref[0] = f.T[0:n_heads, :]


def _cumsum(logf, *, tc, d, emit_bias_lanes, n_heads=N_HEADS):
    b, s, w = logf.shape
    assert s % tc == 0 and tc % LANES == 0 and w == LANES
    if emit_bias_lanes:
        out_specs = [pl.BlockSpec((1, tc, d), lambda i, j: (i, j, 0))] * 2
        out_shape = [jax.ShapeDtypeStruct((b, s, d), BF16)] * 2
    else:
        out_specs = [pl.BlockSpec((1, tc, LANES), lambda i, j: (i, j, 0)),
                     pl.BlockSpec((1, n_heads, tc), lambda i, j: (i, 0, j))]
        out_shape = [jax.ShapeDtypeStruct((b, s, LANES), F32),
                     jax.ShapeDtypeStruct((b, n_heads, s), F32)]
    return pl.pallas_call(
        functools.partial(_cumsum_body, tc=tc, n_heads=n_heads, d=d,
                          emit_bias_lanes=emit_bias_lanes),
        grid=(b, s // tc),
        in_specs=[pl.BlockSpec((1, tc, LANES), lambda i, j: (i, j, 0))],
        out_specs=out_specs, out_shape=out_shape,
        scratch_shapes=[pltpu.VMEM((8, LANES), F32)],
        compiler_params=_params("parallel", "arbitrary"),
        name="logf_cumsum",
    )(logf)


def _fox_body(q_ref, qa_ref, k_ref, ka_ref, vt_ref, o_ref, qh_ref, acc_ref,
              s_ref, m_ref, a_ref, *, tile, n_blocks):
    qw = tile // FOX_QUERY_SPLIT
    chains = [(hh, qs) for hh in range(HEADS_PER_BLOCK) for qs in range(FOX_QUERY_SPLIT)]
    lower = _lower_lanes((tile, LANES))
    ones = jnp.ones((HEAD_DIM, tile), BF16)
    block_rows = lambda i: pl.ds(pl.multiple_of(i * tile, tile), tile)

    def load_queries(i):
        q = q_ref[0, block_rows(i), :]
        qa = qa_ref[0, block_rows(i), :]
        qh_ref[0] = jnp.where(lower, q, qa)
        qh_ref[1] = jnp.where(lower, qa, q)

    def score_step(j, dst, src, diagonal=False):
        off = pl.multiple_of(j * tile, tile)
        kb = k_ref[0, pl.ds(off, tile), :]
        ka = ka_ref[0, pl.ds(off, tile), :]
        kh = (jnp.where(lower, kb, ka), jnp.where(lower, ka, kb))
        for c, (hh, qs) in enumerate(chains):
            s = _nt_dot(kh[hh], qh_ref[hh, qs * qw:(qs + 1) * qw, :])
            if diagonal:
                key = lax.broadcasted_iota(jnp.int32, s.shape, 0)
                qry = lax.broadcasted_iota(jnp.int32, s.shape, 1) + qs * qw
                s = jnp.where(key <= qry, s, NEG_INF)
            m_prev = jnp.full((1, qw), NEG_INF, F32) if src is None else m_ref[src, c]
            m_new = jnp.maximum(m_prev, jnp.max(s, axis=0, keepdims=True))
            s_ref[dst, c] = s
            m_ref[dst, c] = m_new
            a_ref[dst, c] = jnp.exp(m_prev - m_new)

    def value_step(j, src, acc):
        off = pl.multiple_of(j * tile, tile)
        vt = vt_ref[0, :, pl.ds(off, tile)]
        vh = (jnp.concatenate([vt[:HEAD_DIM], ones], axis=0),
              jnp.concatenate([ones, vt[HEAD_DIM:]], axis=0))
        for c, (hh, qs) in enumerate(chains):
            pt = jnp.exp(s_ref[src, c] - m_ref[src, c]).astype(BF16)
            cols = slice(qs * qw, (qs + 1) * qw)
            acc_ref[acc, hh, :, cols] = (a_ref[src, c] * acc_ref[acc, hh, :, cols]
                                         + jnp.dot(vh[hh], pt, preferred_element_type=F32))

    def finish(i, acc):
        o0 = acc_ref[acc, 0, :HEAD_DIM, :] / acc_ref[acc, 0, HEAD_DIM:, :]
        o1 = acc_ref[acc, 1, HEAD_DIM:, :] / acc_ref[acc, 1, :HEAD_DIM, :]
        o_ref[0, block_rows(i), :] = jnp.concatenate([o0, o1], axis=0).T.astype(o_ref.dtype)

    def steady(i, acc, first, second, count):
        block_of_step = lambda t: jnp.where(t == 0, i, t - 1)

        def two_steps(pp):
            t = 2 * pp
            score_step(t, second, first)
            value_step(block_of_step(t), first, acc)
            score_step(t + 1, first, second)
            value_step(t, second, acc)

        def four_steps(u, carry):
            two_steps(2 * u)
            two_steps(2 * u + 1)
            return carry

        lax.fori_loop(0, count // 2, four_steps, 0)

        @pl.when(count % 2 == 1)
        def _():
            two_steps(count - 1)

    def pair_of_blocks(r, carry):
        even, odd = 2 * r, 2 * r + 1
        load_queries(even)
        acc_ref[0] = jnp.zeros(acc_ref.shape[1:], F32)
        score_step(even, 0, None, diagonal=True)
        value_step(jnp.maximum(even - 2, 0), 2, 1)
        finish(jnp.maximum(even - 1, 0), 1)
        steady(even, 0, 0, 1, r)
        load_queries(odd)
        acc_ref[1] = jnp.zeros(acc_ref.shape[1:], F32)
        score_step(odd, 1, None, diagonal=True)
        value_step(jnp.where(even == 0, even, even - 1), 0, 0)
        finish(even, 0)
        steady(odd, 1, 1, 2, r)
        score_step(odd - 1, 2, 1)
        value_step(jnp.where(odd == 1, odd, odd - 2), 1, 1)
        return carry

    s_ref[2] = jnp.zeros(s_ref.shape[1:], F32)
    m_ref[2] = jnp.zeros(m_ref.shape[1:], F32)
    a_ref[2] = jnp.zeros(a_ref.shape[1:], F32)
    acc_ref[1] = jnp.zeros(acc_ref.shape[1:], F32)
    lax.fori_loop(0, n_blocks // 2, pair_of_blocks, 0)
    value_step(n_blocks - 2, 2, 1)
    finish(n_blocks - 1, 1)


def _fox_attn_prompt(q, qa, k, ka, vt):
    b, s, d = q.shape
    tile = FOX_TILE
    assert s % (2 * tile) == 0
    pairs = d // LANES
    n_chains = HEADS_PER_BLOCK * FOX_QUERY_SPLIT
    qw = tile // FOX_QUERY_SPLIT
    seq = pl.BlockSpec((1, s, LANES), lambda i, p: (i, 0, p))
    return pl.pallas_call(
        functools.partial(_fox_body, tile=tile, n_blocks=s // tile),
        grid=(b, pairs),
        in_specs=[seq, seq, seq, seq, pl.BlockSpec((1, LANES, s), lambda i, p: (i, p, 0))],
        out_specs=seq,
        out_shape=jax.ShapeDtypeStruct((b, s, d), BF16),
        scratch_shapes=[pltpu.VMEM((HEADS_PER_BLOCK, tile, LANES), BF16),
                        pltpu.VMEM((2, HEADS_PER_BLOCK, LANES, tile), F32),
                        pltpu.VMEM((3, n_chains, tile, qw), F32),
                        pltpu.VMEM((3, n_chains, 1, qw), F32),
                        pltpu.VMEM((3, n_chains, 1, qw), F32)],
        compiler_params=_params("parallel", "parallel"),
        name="fox_attn",
    )(q, qa, k, ka, vt)


def _trunk(x, mods, weights, a_mix, b_mix, *, bb, ts, keep, v_name):
    b, s, d = x.shape
    depth = mods.shape[0]
    transposed = v_name == "vt"
    k_name, v32_name = ("k32t", "v32t") if transposed else ("k32", "v32")
    a_state, b_state, b_logf = None, None, []
    stacked = lambda st: {k_name: jnp.stack([e[k_name] for e in st]),
                          v32_name: jnp.stack([e[v32_name] for e in st])}
    a_list, b_list = [], []
    for l in range(depth):
        mod = mods[l]
        g, w = weights["g_mix"][l], weights["w_qkv"][l]
        if l % 2 == 0:
            kw = dict(state=a_state) if transposed else {}
            if keep == s:
                proj = kv32 = _qkv_proj(x, mod, g, w, bb=bb, ts=ts,
                                        emit=("q", "k", v_name, k_name, v32_name), **kw)
            else:
                proj = _qkv_proj(x, mod, g, w, bb=bb, ts=ts, emit=("q", "k", v_name))
                kv32 = _qkv_proj(x, mod, g, w, bb=bb, ts=min(ts, keep), emit=(k_name, v32_name),
                                 row_start=s - keep, rows=keep, **kw)
            a_state = kv32
            a_list.append(kv32)
            o = a_mix(l // 2, proj)
        else:
            ib = l // 2
            kw = dict(state=b_state) if transposed else {}
            ts_b = ts // 2 if transposed and b_state is not None else ts
            proj = _qkv_proj(x, mod, g, w, weights["w_fgate"][ib], weights["b_fgate"][ib],
                             bb=bb, ts=ts_b, emit=("q", "k", v_name, k_name, v32_name, "logf"),
                             **kw)
            b_state = proj
            b_list.append(proj)
            b_logf.append(proj["logf"][:, :, :N_HEADS])
            o = b_mix(ib, proj)
        g_out = weights["g_out"] if l == depth - 1 else None
        x = _post_attn(o, x, mod, weights["w_o"][l], weights["g_ffn"][l], weights["w_gu"][l],
                       weights["w_down"][l], g_out, bb=bb, ts=ts)
    if not transposed:
        a_state, b_state = stacked(a_list), stacked(b_list)

    def heads(a):
        if transposed:
            n, bsz, _, r = a.shape
            return a.reshape(n, bsz, N_HEADS, HEAD_DIM, r).transpose(0, 1, 4, 2, 3)
        return a.reshape(a.shape[:3] + (N_HEADS, HEAD_DIM))

    return (x, heads(a_state[k_name]), heads(a_state[v32_name]),
            heads(b_state[k_name]), heads(b_state[v32_name]), jnp.stack(b_logf))


def kernel(x_prompt, x_sample, cache_a_k, cache_a_v, cache_b_k, cache_b_v, cache_b_logf,
           c_prompt, c_sample, w_mod, b_mod, g_mix, g_ffn, w_qkv, w_o, rel_bias,
           w_fgate, b_fgate, w_gu, w_down, g_out):
    bp, sp, d = x_prompt.shape
    bs, t_new, _ = x_sample.shape
    depth = w_mod.shape[0]
    n_b = w_fgate.shape[0]
    assert d == N_HEADS * HEAD_DIM
    keep = min(A_PAST, sp)

    weights = {
        "g_mix": g_mix, "g_ffn": g_ffn, "g_out": g_out,
        "w_qkv": w_qkv.astype(BF16), "w_o": w_o.astype(BF16),
        "w_gu": w_gu.astype(BF16), "w_down": w_down.astype(BF16),
        "w_fgate": jnp.pad(w_fgate, ((0, 0), (0, 0), (0, LANES - N_HEADS))).astype(BF16),
        "b_fgate": jnp.pad(b_fgate, ((0, 0), (0, LANES - N_HEADS))).reshape(n_b, 1, LANES),
    }

    c_all = jnp.concatenate([c_prompt, c_sample], axis=0)
    rows = -(-c_all.shape[0] // 16) * 16
    c_all = jnp.pad(c_all, ((0, rows - c_all.shape[0]), (0, 0)))
    mods = _modulation(c_all, w_mod, b_mod)
    mods_p = mods[:, :bp].reshape(depth, bp, 6, d)
    mods_s = mods[:, bp:bp + bs].reshape(depth, bs, 6, d)

    bias = [_rel_bias_tile(rel_bias[i]) for i in range(rel_bias.shape[0])]

    def a_prompt(i, proj):
        return _chunk_attn_prompt(proj["q"], proj["k"], proj["vt"], bias[i][1])

    def b_prompt(i, proj):
        qa, ka = _cumsum(proj["logf"], tc=FOX_TILE, d=d, emit_bias_lanes=True)
        return _fox_attn_prompt(proj["q"], qa, proj["k"], ka, proj["vt"])

    prompt_out = _trunk(x_prompt, mods_p, weights, a_prompt, b_prompt, bb=1, ts=ROW_TILE,
                              keep=keep, v_name="vt")

    by_feature = lambda c: c.transpose(0, 1, 3, 4, 2).reshape(c.shape[0], bs, d, c.shape[2])
    ca_k, ca_v = by_feature(cache_a_k), by_feature(cache_a_v)
    cb_k, cb_v = by_feature(cache_b_k), by_feature(cache_b_v)
    past = cache_b_k.shape[2]
    total = -(-(past + t_new) // SAMPLE_CUMSUM_TILE) * SAMPLE_CUMSUM_TILE

    def a_sample(i, proj):
        return _sample_attn(proj["q"], proj["k"], proj["v"], ca_k, ca_v, i, bias=bias[i][0])

    def b_sample(i, proj):
        lf_cache = jnp.pad(cache_b_logf[i].astype(F32), ((0, 0), (0, 0), (0, LANES - N_HEADS)))
        lf = jnp.concatenate([lf_cache, proj["logf"]], axis=1)
        lf = jnp.pad(lf, ((0, 0), (0, total - (past + t_new)), (0, 0)))
        f_col, f_row = _cumsum(lf, tc=SAMPLE_CUMSUM_TILE, d=d, emit_bias_lanes=False)
        return _sample_attn(proj["q"], proj["k"], proj["v"], cb_k, cb_v, i,
                            f_col_new=f_col[:, past:past + t_new], f_row=f_row)

    sample_out = _trunk(x_sample, mods_s, weights, a_sample, b_sample, bb=bs, ts=t_new,
                              keep=t_new, v_name="v")

    return (prompt_out[0], sample_out[0]) + tuple(prompt_out[1:]) + tuple(sample_out[1:])
```

```python
import functools

import jax
import jax.numpy as jnp
from jax import lax
from jax.experimental import pallas as pl
from jax.experimental.pallas import tpu as pltpu

F32 = jnp.float32
BF16 = jnp.bfloat16

N_HEADS = 16
HEAD_DIM = 64
CHUNK = 64
N_PAST_CHUNKS = 8
A_PAST = N_PAST_CHUNKS * CHUNK
MAX_REL = 256
RMS_EPS = 1e-6
NEG_INF = -1e30
ATTN_SCALE = HEAD_DIM ** -0.5

LANES = 128
HEADS_PER_BLOCK = LANES // HEAD_DIM
VMEM_LIMIT_BYTES = 48 * 1024 * 1024

ROW_TILE = 512
A_TILE = 256
A_WIN = A_TILE + A_PAST
FOX_TILE = 512
FOX_QUERY_SPLIT = 2
FF_CHUNK = 256
REL_PAD = 384
G_WIDTH = 1024
SAMPLE_CUMSUM_TILE = 384


def _params(*sem):
    return pltpu.CompilerParams(dimension_semantics=sem, vmem_limit_bytes=VMEM_LIMIT_BYTES)


def _resident(shape):
    zeros = (0,) * len(shape)
    return pl.BlockSpec(shape, lambda *_: zeros, pipeline_mode=pl.Buffered(1))


def _nt_dot(a, b):
    return lax.dot_general(a, b, (((1,), (1,)), ((), ())), preferred_element_type=F32)


def _transpose_bf16(x):
    return x.astype(F32).T.astype(BF16)


def _modulated_norm(x, g, scale, shift):
    ms = jnp.mean(x * x, axis=-1, keepdims=True)
    y = x * lax.rsqrt(ms + RMS_EPS) * g
    return y * (1.0 + scale) + shift


def _split3(x):
    hi = x.astype(BF16)
    r1 = x - hi.astype(F32)
    mid = r1.astype(BF16)
    lo = (r1 - mid.astype(F32)).astype(BF16)
    return hi, mid, lo


def _lower_lanes(shape):
    return lax.broadcasted_iota(jnp.int32, shape, len(shape) - 1) < HEAD_DIM


def _mod_body(c_ref, w_ref, b_ref, o_ref):
    cs = jax.nn.silu(c_ref[...]).astype(BF16)
    o_ref[0] = jnp.dot(cs, w_ref[0].astype(BF16), preferred_element_type=F32) + b_ref[0]


def _modulation(c, w_mod, b_mod):
    depth, d, n = w_mod.shape
    rows = c.shape[0]
    tn = 1536
    assert n % tn == 0 and rows % 16 == 0
    return pl.pallas_call(
        _mod_body,
        grid=(depth, n // tn),
        in_specs=[pl.BlockSpec((rows, d), lambda l, j: (0, 0)),
                  pl.BlockSpec((1, d, tn), lambda l, j: (l, 0, j)),
                  pl.BlockSpec((1, 1, tn), lambda l, j: (l, 0, j))],
        out_specs=pl.BlockSpec((1, rows, tn), lambda l, j: (l, 0, j)),
        out_shape=jax.ShapeDtypeStruct((depth, rows, n), F32),
        compiler_params=_params("parallel", "parallel"),
        name="modulation",
    )(c, w_mod, b_mod.reshape(depth, 1, n))


def _qkv_body(*refs, d, emit, n_prev):
    x_ref, mod_ref, g_ref, w_ref = refs[:4]
    pos = 4
    if "logf" in emit:
        wf_ref, bf_ref = refs[4:6]
        pos = 6
    prev = dict(zip(("k32t", "v32t"), refs[pos:pos + (2 if n_prev else 0)]))
    out = dict(zip(emit, refs[pos + len(prev):]))
    for name, prev_ref in prev.items():
        out[name][0:n_prev] = prev_ref[...]
    x = x_ref[...]
    bb, ts, _ = x.shape
    h = _modulated_norm(x, g_ref[...], mod_ref[:, 1:2, :], mod_ref[:, 0:1, :])
    hb = h.reshape(bb * ts, d).astype(BF16)

    def put(name, val, dtype):
        out[name][...] = val.reshape(bb, ts, val.shape[-1]).astype(dtype)

    if "q" in emit or "qt" in emit:
        q = jnp.dot(hb, w_ref[:, 0:d], preferred_element_type=F32) * ATTN_SCALE
    if "q" in emit:
        put("q", q, BF16)
    if "qt" in emit:
        out["qt"][0] = q.T.astype(BF16)
    k = jnp.dot(hb, w_ref[:, d:2 * d], preferred_element_type=F32)
    v = jnp.dot(hb, w_ref[:, 2 * d:3 * d], preferred_element_type=F32)
    if "k" in emit:
        put("k", k, BF16)
    if "v" in emit:
        put("v", v, BF16)
    if "vt" in emit or "v32t" in emit:
        v_t = v.T
    if "vt" in emit:
        out["vt"][0] = v_t.astype(BF16)
    if "v32t" in emit:
        out["v32t"][n_prev, 0] = v_t
    if "k32t" in emit:
        out["k32t"][n_prev, 0] = k.T
    if "k32" in emit:
        put("k32", k, F32)
    if "v32" in emit:
        put("v32", v, F32)
    if "logf" in emit:
        z = jnp.dot(hb, wf_ref[...], preferred_element_type=F32) + bf_ref[...]
        put("logf", jax.nn.log_sigmoid(z), F32)


def _qkv_proj(x, mod, g, w_qkv, wf=None, bf=None, *, bb, ts, emit, row_start=0, rows=None,
              state=None):
    b, s, d = x.shape
    rows = s if rows is None else rows
    assert rows % ts == 0 and row_start % ts == 0 and b % bb == 0
    assert ts % 16 == 0 or bb == 1
    assert bb == 1 or not {"qt", "vt", "k32t", "v32t"} & set(emit)
    off = row_start // ts
    n_prev = 0 if state is None else state["k32t"].shape[0]
    assert ("logf" in emit) == (wf is not None)
    row_spec = lambda width: pl.BlockSpec((bb, ts, width), lambda i, j: (i, j, 0))
    in_specs = [pl.BlockSpec((bb, ts, d), lambda i, j: (i, j + off, 0)),
                pl.BlockSpec((bb, 6, d), lambda i, j: (i, 0, 0)),
                _resident((1, d)),
                _resident((d, 3 * d))]
    args = [x, mod, g.reshape(1, d), w_qkv]
    if wf is not None:
        in_specs += [_resident((d, LANES)), _resident((1, LANES))]
        args += [wf, bf]
    kinds = {
        "q": (row_spec(d), (b, rows, d), BF16), "k": (row_spec(d), (b, rows, d), BF16),
        "v": (row_spec(d), (b, rows, d), BF16),
        "vt": (pl.BlockSpec((1, d, ts), lambda i, j: (i, 0, j)), (b, d, rows), BF16),
        "qt": (pl.BlockSpec((1, d, ts), lambda i, j: (i, 0, j)), (b, d, rows), BF16),
        "k32": (row_spec(d), (b, rows, d), F32), "v32": (row_spec(d), (b, rows, d), F32),
        "logf": (row_spec(LANES), (b, rows, LANES), F32),
    }
    stack_spec = lambda n: pl.BlockSpec((n, 1, d, ts), lambda i, j: (0, i, 0, j))
    kinds["k32t"] = kinds["v32t"] = (stack_spec(n_prev + 1), (n_prev + 1, b, d, rows), F32)
    if n_prev:
        in_specs += [stack_spec(n_prev)] * 2
        args += [state["k32t"], state["v32t"]]
    outs = pl.pallas_call(
        functools.partial(_qkv_body, d=d, emit=emit, n_prev=n_prev),
        grid=(b // bb, rows // ts),
        in_specs=in_specs,
        out_specs=[kinds[e][0] for e in emit],
        out_shape=[jax.ShapeDtypeStruct(kinds[e][1], kinds[e][2]) for e in emit],
        compiler_params=_params("parallel", "parallel"),
        name="norm_qkv",
    )(*args)
    return dict(zip(emit, outs))


def _post_attn_body(*refs, d_ff, final_norm):
    o_ref, x_ref, mod_ref, wo_ref, g_ref, wgu_ref, wd_ref = refs[:7]
    out_ref = refs[-1]
    x = x_ref[...]
    bb, ts, d = x.shape
    mod = lambda r: mod_ref[:, r:r + 1, :]
    y = jnp.dot(o_ref[...].reshape(bb * ts, d), wo_ref[...], preferred_element_type=F32)
    x1 = x + mod(2) * y.reshape(bb, ts, d)
    h = _modulated_norm(x1, g_ref[...], mod(4), mod(3))
    hb = h.reshape(bb * ts, d).astype(BF16)
    acc = jnp.zeros((bb * ts, d), F32)
    for c in range(d_ff // FF_CHUNK):
        lo = c * FF_CHUNK
        gate = jnp.dot(hb, wgu_ref[:, lo:lo + FF_CHUNK], preferred_element_type=F32)
        up = jnp.dot(hb, wgu_ref[:, d_ff + lo:d_ff + lo + FF_CHUNK], preferred_element_type=F32)
        act = (jax.nn.silu(gate) * up).astype(BF16)
        acc = acc + jnp.dot(act, wd_ref[lo:lo + FF_CHUNK, :], preferred_element_type=F32)
    x2 = x1 + mod(5) * acc.reshape(bb, ts, d)
    if final_norm:
        gout_ref = refs[7]
        ms = jnp.mean(x2 * x2, axis=-1, keepdims=True)
        x2 = x2 * lax.rsqrt(ms + RMS_EPS) * gout_ref[...]
    out_ref[...] = x2


def _post_attn(o, x, mod, w_o, g, w_gu, w_down, g_out=None, *, bb, ts):
    b, s, d = x.shape
    d_ff = w_down.shape[0]
    assert s % ts == 0 and b % bb == 0 and d_ff % FF_CHUNK == 0
    assert ts % 16 == 0 or bb == 1
    row = pl.BlockSpec((bb, ts, d), lambda i, j: (i, j, 0))
    in_specs = [row, row, pl.BlockSpec((bb, 6, d), lambda i, j: (i, 0, 0)), _resident((d, d)),
                _resident((1, d)), _resident((d, 2 * d_ff)), _resident((d_ff, d))]
    args = [o, x, mod, w_o, g.reshape(1, d), w_gu, w_down]
    if g_out is not None:
        in_specs.append(_resident((1, d)))
        args.append(g_out.reshape(1, d))
    return pl.pallas_call(
        functools.partial(_post_attn_body, d_ff=d_ff, final_norm=g_out is not None),
        grid=(b // bb, s // ts),
        in_specs=in_specs, out_specs=row,
        out_shape=jax.ShapeDtypeStruct((b, s, d), F32),
        compiler_params=_params("parallel", "parallel"),
        name="oproj_ffn",
    )(*args)


def _bias_row_body(tab_ref, g_ref):
    u = lax.broadcasted_iota(jnp.int32, (REL_PAD, G_WIDTH), 1)
    r = lax.broadcasted_iota(jnp.int32, (REL_PAD, G_WIDTH), 0)
    rel = A_PAST + (A_TILE - 1) - u
    idx = jnp.clip(rel, -(CHUNK - 1), MAX_REL) + (CHUNK - 1)
    onehot = jnp.where(r == idx, 1.0, 0.0).astype(BF16)
    hi, mid, lo = _split3(tab_ref[...])
    g_ref[...] = (jnp.dot(hi, onehot, preferred_element_type=F32)
                  + jnp.dot(mid, onehot, preferred_element_type=F32)
                  + jnp.dot(lo, onehot, preferred_element_type=F32))


def _bias_tile_body(g_ref, o_ref, ot_ref):
    x = jnp.broadcast_to(g_ref[0], (A_TILE, G_WIDTH))
    rolled = pltpu.roll(x, G_WIDTH - (A_TILE - 1), 1, stride=1, stride_axis=0)
    val = rolled[:, :A_WIN]
    qc = lax.broadcasted_iota(jnp.int32, (A_TILE, A_WIN), 0) // CHUNK
    kc = lax.broadcasted_iota(jnp.int32, (A_TILE, A_WIN), 1) // CHUNK
    band = (kc >= qc) & (kc <= qc + N_PAST_CHUNKS)
    tile = jnp.where(band, val, NEG_INF)
    o_ref[0] = tile
    ot_ref[0] = tile.T


def _rel_bias_tile(table):
    h, rel = table.shape
    assert rel == CHUNK + MAX_REL and rel <= REL_PAD and A_WIN + A_TILE - 1 <= G_WIDTH
    tab = jnp.pad(table, ((0, 0), (0, REL_PAD - rel)))
    gen = pl.pallas_call(
        _bias_row_body,
        out_shape=jax.ShapeDtypeStruct((h, G_WIDTH), F32),
        name="rel_bias_row",
    )(tab)
    return pl.pallas_call(
        _bias_tile_body,
        grid=(h,),
        in_specs=[pl.BlockSpec((1, 1, G_WIDTH), lambda i: (i, 0, 0))],
        out_specs=[pl.BlockSpec((1, A_TILE, A_WIN), lambda i: (i, 0, 0)),
                   pl.BlockSpec((1, A_WIN, A_TILE), lambda i: (i, 0, 0))],
        out_shape=[jax.ShapeDtypeStruct((h, A_TILE, A_WIN), F32),
                   jax.ShapeDtypeStruct((h, A_WIN, A_TILE), F32)],
        compiler_params=_params("parallel"),
        name="rel_bias_tile",
    )(gen.reshape(h, 1, G_WIDTH))


def _head_select(shape, hh):
    lower = _lower_lanes(shape)
    return lower if hh == 0 else jnp.logical_not(lower)


def _chunk_attn_body(q_ref, k_ref, vt_ref, bias_ref, o_ref, s0_ref, s1_ref, m0_ref, m1_ref,
                     *, n_tiles):
    sets = ((s0_ref, m0_ref), (s1_ref, m1_ref))

    def head_queries(qt):
        zero = jnp.zeros((HEAD_DIM, qt.shape[1]), qt.dtype)
        return (jnp.concatenate([qt[:HEAD_DIM], zero], axis=0),
                jnp.concatenate([zero, qt[HEAD_DIM:]], axis=0))

    def head_values(vt):
        ones = jnp.ones((HEAD_DIM, vt.shape[1]), BF16)
        return (jnp.concatenate([vt[:HEAD_DIM], ones], axis=0),
                jnp.concatenate([ones, vt[HEAD_DIM:]], axis=0))

    def finish(accs):
        o0 = accs[0][:HEAD_DIM] / accs[0][HEAD_DIM:]
        o1 = accs[1][HEAD_DIM:] / accs[1][:HEAD_DIM]
        return jnp.concatenate([o0, o1], axis=0).T.astype(o_ref.dtype)

    def edge_tile(t):
        nk = (t + 1) * A_TILE
        qm = head_queries(q_ref[0, :, t * A_TILE:(t + 1) * A_TILE])
        kw = k_ref[0, 0:nk, :]
        vh = head_values(vt_ref[0, :, 0:nk])
        accs = []
        for hh in range(HEADS_PER_BLOCK):
            s = (jnp.dot(kw, qm[hh], preferred_element_type=F32)
                 + bias_ref[hh, A_WIN - nk:A_WIN, :])
            pt = jnp.exp(s - jnp.max(s, axis=0, keepdims=True)).astype(BF16)
            accs.append(jnp.dot(vh[hh], pt, preferred_element_type=F32))
        o_ref[0, t * A_TILE:(t + 1) * A_TILE, :] = finish(accs)

    def score_step(t, dst):
        s_ref, m_ref = sets[dst]
        qm = head_queries(q_ref[0, :, pl.ds(pl.multiple_of(t * A_TILE, A_TILE), A_TILE)])
        kw = k_ref[0, pl.ds(pl.multiple_of((t - 2) * A_TILE, A_TILE), A_WIN), :]
        for hh in range(HEADS_PER_BLOCK):
            s = jnp.dot(kw, qm[hh], preferred_element_type=F32) + bias_ref[hh]
            s_ref[hh] = s
            m_ref[hh] = jnp.max(s, axis=0, keepdims=True)

    def value_step(t, src):
        s_ref, m_ref = sets[src]
        vh = head_values(vt_ref[0, :, pl.ds(pl.multiple_of((t - 2) * A_TILE, A_TILE), A_WIN)])
        accs = []
        for hh in range(HEADS_PER_BLOCK):
            pt = jnp.exp(s_ref[hh] - m_ref[hh]).astype(BF16)
            accs.append(jnp.dot(vh[hh], pt, preferred_element_type=F32))
        o_ref[0, pl.ds(pl.multiple_of(t * A_TILE, A_TILE), A_TILE), :] = finish(accs)

    edge_tile(0)
    edge_tile(1)
    score_step(2, 0)

    def two_tiles(u):
        t = 2 + 2 * u
        score_step(t + 1, 1)
        value_step(t, 0)
        score_step(t + 2, 0)
        value_step(t + 1, 1)

    def four_tiles(w, carry):
        two_tiles(2 * w)
        two_tiles(2 * w + 1)
        return carry

    pairs_of_tiles = (n_tiles - 4) // 2
    lax.fori_loop(0, pairs_of_tiles // 2, four_tiles, 0)
    if pairs_of_tiles % 2:
        two_tiles(pairs_of_tiles - 1)
    score_step(n_tiles - 1, 1)
    value_step(n_tiles - 2, 0)
    value_step(n_tiles - 1, 1)


def _chunk_attn_prompt(qt, k, vt, bias_t):
    b, s, d = k.shape
    n_tiles = s // A_TILE
    assert A_WIN == 3 * A_TILE and s % A_TILE == 0 and d % LANES == 0
    assert n_tiles >= 4 and n_tiles % 2 == 0
    pairs = d // LANES
    rows = pl.BlockSpec((1, s, LANES), lambda p, i: (i, 0, p))
    cols = pl.BlockSpec((1, LANES, s), lambda p, i: (i, p, 0))
    return pl.pallas_call(
        functools.partial(_chunk_attn_body, n_tiles=n_tiles),
        grid=(pairs, b),
        in_specs=[cols, rows, cols,
                  pl.BlockSpec((HEADS_PER_BLOCK, A_WIN, A_TILE), lambda p, i: (p, 0, 0))],
        out_specs=rows,
        out_shape=jax.ShapeDtypeStruct((b, s, d), BF16),
        scratch_shapes=[pltpu.VMEM((HEADS_PER_BLOCK, A_WIN, A_TILE), F32)] * 2
        + [pltpu.VMEM((HEADS_PER_BLOCK, 1, A_TILE), F32)] * 2,
        compiler_params=_params("parallel", "parallel"),
        name="chunk_attn",
    )(qt, k, vt, bias_t)


def _sample_attn_body(*refs, n_cache, forget):
    q_ref, kc_ref, vc_ref, kn_ref, vn_ref = refs[:5]
    o_ref = refs[-1]
    t, d = q_ref.shape[1:]
    if forget:
        fq_ref, fk_ref = refs[5:7]
        causal = (lax.broadcasted_iota(jnp.int32, (t, t), 1)
                  <= lax.broadcasted_iota(jnp.int32, (t, t), 0))
    else:
        bias_ref = refs[5]
    for p in range(d // LANES):
        lanes = slice(p * LANES, (p + 1) * LANES)
        q = q_ref[0, :, lanes]
        kc_t = kc_ref[0, 0, lanes, :].astype(BF16)
        vc_t = vc_ref[0, 0, lanes, :].astype(BF16)
        kn = kn_ref[0, :, lanes]
        vn = vn_ref[0, :, lanes]
        outs = []
        for hh in range(HEADS_PER_BLOCK):
            head = HEADS_PER_BLOCK * p + hh
            qm = jnp.where(_head_select(q.shape, hh), q, jnp.zeros_like(q))
            if forget:
                fq = fq_ref[0, :, head:head + 1]
                sc = (jnp.dot(qm, kc_t, preferred_element_type=F32)
                      + (fq - fk_ref[0, head:head + 1, 0:n_cache]))
                sn = _nt_dot(qm, kn) + (fq - fk_ref[0, head:head + 1, n_cache:n_cache + t])
                sn = jnp.where(causal, sn, NEG_INF)
            else:
                sc = (jnp.dot(qm, kc_t, preferred_element_type=F32)
                      + bias_ref[head, :, 0:n_cache])
                sn = _nt_dot(qm, kn) + bias_ref[head, :, n_cache:n_cache + t]
            m = jnp.maximum(jnp.max(sc, axis=-1, keepdims=True),
                            jnp.max(sn, axis=-1, keepdims=True))
            pc = jnp.exp(sc - m)
            pn = jnp.exp(sn - m)
            l = jnp.sum(pc, axis=-1, keepdims=True) + jnp.sum(pn, axis=-1, keepdims=True)
            o = (_nt_dot(pc.astype(BF16), vc_t)
                 + jnp.dot(pn.astype(BF16), vn, preferred_element_type=F32))
            outs.append(o / l)
        o_ref[0, :, lanes] = jnp.where(_lower_lanes(outs[0].shape), outs[0],
                                       outs[1]).astype(o_ref.dtype)


def _sample_attn(q, k, v, k_cache, v_cache, layer, *, bias=None, f_col_new=None, f_row=None):
    b, t, d = q.shape
    n_cache = k_cache.shape[3]
    new = pl.BlockSpec((1, t, d), lambda i: (i, 0, 0))
    cache = pl.BlockSpec((1, 1, d, n_cache), lambda i: (layer, i, 0, 0))
    forget = bias is None
    if forget:
        extra_specs = [pl.BlockSpec((1, t, LANES), lambda i: (i, 0, 0)),
                       pl.BlockSpec((1,) + f_row.shape[1:], lambda i: (i, 0, 0))]
        extra = [f_col_new, f_row]
    else:
        assert n_cache == A_PAST and t <= CHUNK and n_cache + t <= A_WIN
        extra_specs = [pl.BlockSpec((bias.shape[0], t, A_WIN), lambda i: (0, 0, 0))]
        extra = [bias]
    return pl.pallas_call(
        functools.partial(_sample_attn_body, n_cache=n_cache, forget=forget),
        grid=(b,),
        in_specs=[new, cache, cache, new, new] + extra_specs,
        out_specs=new,
        out_shape=jax.ShapeDtypeStruct((b, t, d), BF16),
        compiler_params=_params("parallel"),
        name="sample_attn",
    )(q, k_cache, v_cache, k, v, *extra)


def _bias_lanes(f, d):
    pieces = jnp.concatenate(_split3(f), axis=1)
    col = lax.broadcasted_iota(jnp.int32, (3 * LANES, d), 1)
    row = lax.broadcasted_iota(jnp.int32, (3 * LANES, d), 0)
    lane = col % LANES
    head = HEADS_PER_BLOCK * (col // LANES) + jnp.where(lane >= HEAD_DIM, 0, 1)
    slot = lane % HEAD_DIM
    hit = (row % LANES) == head

    def pick(first):
        sel = jnp.where(hit & (slot == first + row // LANES), 1.0, 0.0).astype(BF16)
        return jnp.dot(pieces, sel, preferred_element_type=F32)

    slot_row = lax.broadcasted_iota(jnp.int32, (1, d), 1) % HEAD_DIM
    ones = lambda first: jnp.where((slot_row >= first) & (slot_row < first + 3), 1.0, 0.0)
    q_side = pick(0) + ones(3)
    k_side = ones(0) - pick(3)
    return q_side, k_side


def _cumsum_body(x_ref, *refs, tc, n_heads, d, emit_bias_lanes):
    carry_ref = refs[-1]

    @pl.when(pl.program_id(1) == 0)
    def _():
        carry_ref[...] = jnp.zeros_like(carry_ref)

    r = lax.broadcasted_iota(jnp.int32, (tc, tc), 0)
    c = lax.broadcasted_iota(jnp.int32, (tc, tc), 1)
    tri = jnp.where(c <= r, 1.0, 0.0).astype(BF16)
    pieces = jnp.concatenate(_split3(x_ref[0]), axis=1)
    sums = jnp.dot(tri, pieces, preferred_element_type=F32)
    f = ((sums[:, 0:LANES] + sums[:, LANES:2 * LANES]) + sums[:, 2 * LANES:3 * LANES]
         + carry_ref[0:1, :])
    carry_ref[0:1, :] = f[tc - 1:tc, :]
    if emit_bias_lanes:
        qa_ref, ka_ref = refs[:2]
        q_side, k_side = _bias_lanes(f, d)
        qa_ref[0] = q_side.T.astype(BF16)
        ka_ref[0] = k_side.astype(BF16)
    else:
        fcol_ref, frow_ref = refs[:2]
        fcol_ref[0] = f
        frow_ref[0] = f.T[0:n_heads, :]


def _cumsum(logf, *, tc, d, emit_bias_lanes, n_heads=N_HEADS):
    b, s, w = logf.shape
    assert s % tc == 0 and tc % LANES == 0 and w == LANES
    if emit_bias_lanes:
        out_specs = [pl.BlockSpec((1, d, tc), lambda i, j: (i, 0, j)),
                     pl.BlockSpec((1, tc, d), lambda i, j: (i, j, 0))]
        out_shape = [jax.ShapeDtypeStruct((b, d, s), BF16), jax.ShapeDtypeStruct((b, s, d), BF16)]
    else:
        out_specs = [pl.BlockSpec((1, tc, LANES), lambda i, j: (i, j, 0)),
                     pl.BlockSpec((1, n_heads, tc), lambda i, j: (i, 0, j))]
        out_shape = [jax.ShapeDtypeStruct((b, s, LANES), F32),
                     jax.ShapeDtypeStruct((b, n_heads, s), F32)]
    return pl.pallas_call(
        functools.partial(_cumsum_body, tc=tc, n_heads=n_heads, d=d,
                          emit_bias_lanes=emit_bias_lanes),
        grid=(b, s // tc),
        in_specs=[pl.BlockSpec((1, tc, LANES), lambda i, j: (i, j, 0))],
        out_specs=out_specs, out_shape=out_shape,
        scratch_shapes=[pltpu.VMEM((8, LANES), F32)],
        compiler_params=_params("parallel", "arbitrary"),
        name="logf_cumsum",
    )(logf)


def _fox_body(q_ref, qa_ref, k_ref, ka_ref, vt_ref, o_ref, qh_ref, acc_ref,
              s_ref, m_ref, a_ref, *, tile, n_blocks):
    qw = tile // FOX_QUERY_SPLIT
    chains = [(hh, qs) for hh in range(HEADS_PER_BLOCK) for qs in range(FOX_QUERY_SPLIT)]
    lower = _lower_lanes((tile, LANES))
    ones = jnp.ones((HEAD_DIM, tile), BF16)
    block_rows = lambda i: pl.ds(pl.multiple_of(i * tile, tile), tile)

    def load_queries(i):
        qt = q_ref[0, :, block_rows(i)]
        qat = qa_ref[0, :, block_rows(i)]
        qh_ref[0] = jnp.concatenate([qt[:HEAD_DIM], qat[HEAD_DIM:]], axis=0)
        qh_ref[1] = jnp.concatenate([qat[:HEAD_DIM], qt[HEAD_DIM:]], axis=0)

    def score_step(j, dst, src, diagonal=False):
        off = pl.multiple_of(j * tile, tile)
        kb = k_ref[0, pl.ds(off, tile), :]
        ka = ka_ref[0, pl.ds(off, tile), :]
        kh = (jnp.where(lower, kb, ka), jnp.where(lower, ka, kb))
        for c, (hh, qs) in enumerate(chains):
            s = jnp.dot(kh[hh], qh_ref[hh, :, qs * qw:(qs + 1) * qw],
                        preferred_element_type=F32)
            if diagonal:
                key = lax.broadcasted_iota(jnp.int32, s.shape, 0)
                qry = lax.broadcasted_iota(jnp.int32, s.shape, 1) + qs * qw
                s = jnp.where(key <= qry, s, NEG_INF)
            m_prev = jnp.full((1, qw), NEG_INF, F32) if src is None else m_ref[src, c]
            m_new = jnp.maximum(m_prev, jnp.max(s, axis=0, keepdims=True))
            s_ref[dst, c] = s
            m_ref[dst, c] = m_new
            a_ref[dst, c] = jnp.exp(m_prev - m_new)

    def value_step(j, src, acc):
        off = pl.multiple_of(j * tile, tile)
        vt = vt_ref[0, :, pl.ds(off, tile)]
        vh = (jnp.concatenate([vt[:HEAD_DIM], ones], axis=0),
              jnp.concatenate([ones, vt[HEAD_DIM:]], axis=0))
        for c, (hh, qs) in enumerate(chains):
            pt = jnp.exp(s_ref[src, c] - m_ref[src, c]).astype(BF16)
            cols = slice(qs * qw, (qs + 1) * qw)
            acc_ref[acc, hh, :, cols] = (a_ref[src, c] * acc_ref[acc, hh, :, cols]
                                         + jnp.dot(vh[hh], pt, preferred_element_type=F32))

    def finish(i, acc):
        o0 = acc_ref[acc, 0, :HEAD_DIM, :] / acc_ref[acc, 0, HEAD_DIM:, :]
        o1 = acc_ref[acc, 1, HEAD_DIM:, :] / acc_ref[acc, 1, :HEAD_DIM, :]
        o_ref[0, block_rows(i), :] = jnp.concatenate([o0, o1], axis=0).T.astype(o_ref.dtype)

    def steady(i, acc, first, second, count):
        block_of_step = lambda t: jnp.where(t == 0, i, t - 1)

        def two_steps(pp):
            t = 2 * pp
            score_step(t, second, first)
            value_step(block_of_step(t), first, acc)
            score_step(t + 1, first, second)
            value_step(t, second, acc)

        def four_steps(u, carry):
            two_steps(2 * u)
            two_steps(2 * u + 1)
            return carry

        lax.fori_loop(0, count // 2, four_steps, 0)

        @pl.when(count % 2 == 1)
        def _():
            two_steps(count - 1)

    def pair_of_blocks(r, carry):
        even, odd = 2 * r, 2 * r + 1
        load_queries(even)
        acc_ref[0] = jnp.zeros(acc_ref.shape[1:], F32)
        score_step(even, 0, None, diagonal=True)
        value_step(jnp.maximum(even - 2, 0), 2, 1)
        finish(jnp.maximum(even - 1, 0), 1)
        steady(even, 0, 0, 1, r)
        load_queries(odd)
        acc_ref[1] = jnp.zeros(acc_ref.shape[1:], F32)
        score_step(odd, 1, None, diagonal=True)
        value_step(jnp.where(even == 0, even, even - 1), 0, 0)
        finish(even, 0)
        steady(odd, 1, 1, 2, r)
        score_step(odd - 1, 2, 1)
        value_step(jnp.where(odd == 1, odd, odd - 2), 1, 1)
        return carry

    s_ref[2] = jnp.zeros(s_ref.shape[1:], F32)
    m_ref[2] = jnp.zeros(m_ref.shape[1:], F32)
    a_ref[2] = jnp.zeros(a_ref.shape[1:], F32)
    acc_ref[1] = jnp.zeros(acc_ref.shape[1:], F32)
    lax.fori_loop(0, n_blocks // 2, pair_of_blocks, 0)
    value_step(n_blocks - 2, 2, 1)
    finish(n_blocks - 1, 1)


def _fox_attn_prompt(qt, qat, k, ka, vt):
    b, s, d = k.shape
    tile = FOX_TILE
    assert s % (2 * tile) == 0
    pairs = d // LANES
    n_chains = HEADS_PER_BLOCK * FOX_QUERY_SPLIT
    qw = tile // FOX_QUERY_SPLIT
    seq = pl.BlockSpec((1, s, LANES), lambda i, p: (i, 0, p))
    seq_t = pl.BlockSpec((1, LANES, s), lambda i, p: (i, p, 0))
    return pl.pallas_call(
        functools.partial(_fox_body, tile=tile, n_blocks=s // tile),
        grid=(b, pairs),
        in_specs=[seq_t, seq_t, seq, seq, seq_t],
        out_specs=seq,
        out_shape=jax.ShapeDtypeStruct((b, s, d), BF16),
        scratch_shapes=[pltpu.VMEM((HEADS_PER_BLOCK, LANES, tile), BF16),
                        pltpu.VMEM((2, HEADS_PER_BLOCK, LANES, tile), F32),
                        pltpu.VMEM((3, n_chains, tile, qw), F32),
                        pltpu.VMEM((3, n_chains, 1, qw), F32),
                        pltpu.VMEM((3, n_chains, 1, qw), F32)],
        compiler_params=_params("parallel", "parallel"),
        name="fox_attn",
    )(qt, qat, k, ka, vt)


def _trunk(x, mods, weights, a_mix, b_mix, *, bb, ts, keep, v_name):
    b, s, d = x.shape
    depth = mods.shape[0]
    transposed = v_name == "vt"
    k_name, v32_name = ("k32t", "v32t") if transposed else ("k32", "v32")
    q_name = "qt" if transposed else "q"
    a_state, b_state, b_logf = None, None, []
    stacked = lambda st: {k_name: jnp.stack([e[k_name] for e in st]),
                          v32_name: jnp.stack([e[v32_name] for e in st])}
    a_list, b_list = [], []
    for l in range(depth):
        mod = mods[l]
        g, w = weights["g_mix"][l], weights["w_qkv"][l]
        if l % 2 == 0:
            kw = dict(state=a_state) if transposed else {}
            if keep == s:
                proj = kv32 = _qkv_proj(x, mod, g, w, bb=bb, ts=ts,
                                        emit=(q_name, "k", v_name, k_name, v32_name), **kw)
            else:
                proj = _qkv_proj(x, mod, g, w, bb=bb, ts=ts, emit=(q_name, "k", v_name))
                kv32 = _qkv_proj(x, mod, g, w, bb=bb, ts=min(ts, keep), emit=(k_name, v32_name),
                                 row_start=s - keep, rows=keep, **kw)
            a_state = kv32
            a_list.append(kv32)
            o = a_mix(l // 2, proj)
        else:
            ib = l // 2
            kw = dict(state=b_state) if transposed else {}
            ts_b = ts // 2 if transposed and b_state is not None else ts
            proj = _qkv_proj(x, mod, g, w, weights["w_fgate"][ib], weights["b_fgate"][ib],
                             bb=bb, ts=ts_b, emit=(q_name, "k", v_name, k_name, v32_name, "logf"),
                             **kw)
            b_state = proj
            b_list.append(proj)
            b_logf.append(proj["logf"][:, :, :N_HEADS])
            o = b_mix(ib, proj)
        g_out = weights["g_out"] if l == depth - 1 else None
        x = _post_attn(o, x, mod, weights["w_o"][l], weights["g_ffn"][l], weights["w_gu"][l],
                       weights["w_down"][l], g_out, bb=bb, ts=ts)
    if not transposed:
        a_state, b_state = stacked(a_list), stacked(b_list)

    def heads(a):
        if transposed:
            n, bsz, _, r = a.shape
            return a.reshape(n, bsz, N_HEADS, HEAD_DIM, r).transpose(0, 1, 4, 2, 3)
        return a.reshape(a.shape[:3] + (N_HEADS, HEAD_DIM))

    return (x, heads(a_state[k_name]), heads(a_state[v32_name]),
            heads(b_state[k_name]), heads(b_state[v32_name]), jnp.stack(b_logf))


def kernel(x_prompt, x_sample, cache_a_k, cache_a_v, cache_b_k, cache_b_v, cache_b_logf,
           c_prompt, c_sample, w_mod, b_mod, g_mix, g_ffn, w_qkv, w_o, rel_bias,
           w_fgate, b_fgate, w_gu, w_down, g_out):
    bp, sp, d = x_prompt.shape
    bs, t_new, _ = x_sample.shape
    depth = w_mod.shape[0]
    n_b = w_fgate.shape[0]
    assert d == N_HEADS * HEAD_DIM
    keep = min(A_PAST, sp)

    weights = {
        "g_mix": g_mix, "g_ffn": g_ffn, "g_out": g_out,
        "w_qkv": w_qkv.astype(BF16), "w_o": w_o.astype(BF16),
        "w_gu": w_gu.astype(BF16), "w_down": w_down.astype(BF16),
        "w_fgate": jnp.pad(w_fgate, ((0, 0), (0, 0), (0, LANES - N_HEADS))).astype(BF16),
        "b_fgate": jnp.pad(b_fgate, ((0, 0), (0, LANES - N_HEADS))).reshape(n_b, 1, LANES),
    }

    c_all = jnp.concatenate([c_prompt, c_sample], axis=0)
    rows = -(-c_all.shape[0] // 16) * 16
    c_all = jnp.pad(c_all, ((0, rows - c_all.shape[0]), (0, 0)))
    mods = _modulation(c_all, w_mod, b_mod)
    mods_p = mods[:, :bp].reshape(depth, bp, 6, d)
    mods_s = mods[:, bp:bp + bs].reshape(depth, bs, 6, d)

    bias = [_rel_bias_tile(rel_bias[i]) for i in range(rel_bias.shape[0])]

    def a_prompt(i, proj):
        return _chunk_attn_prompt(proj["qt"], proj["k"], proj["vt"], bias[i][1])

    def b_prompt(i, proj):
        qa, ka = _cumsum(proj["logf"], tc=FOX_TILE, d=d, emit_bias_lanes=True)
        return _fox_attn_prompt(proj["qt"], qa, proj["k"], ka, proj["vt"])

    prompt_out = _trunk(x_prompt, mods_p, weights, a_prompt, b_prompt, bb=1, ts=ROW_TILE,
                              keep=keep, v_name="vt")

    by_feature = lambda c: c.transpose(0, 1, 3, 4, 2).reshape(c.shape[0], bs, d, c.shape[2])
    ca_k, ca_v = by_feature(cache_a_k), by_feature(cache_a_v)
    cb_k, cb_v = by_feature(cache_b_k), by_feature(cache_b_v)
    past = cache_b_k.shape[2]
    total = -(-(past + t_new) // SAMPLE_CUMSUM_TILE) * SAMPLE_CUMSUM_TILE

    def a_sample(i, proj):
        return _sample_attn(proj["q"], proj["k"], proj["v"], ca_k, ca_v, i, bias=bias[i][0])

    def b_sample(i, proj):
        lf_cache = jnp.pad(cache_b_logf[i].astype(F32), ((0, 0), (0, 0), (0, LANES - N_HEADS)))
        lf = jnp.concatenate([lf_cache, proj["logf"]], axis=1)
        lf = jnp.pad(lf, ((0, 0), (0, total - (past + t_new)), (0, 0)))
        f_col, f_row = _cumsum(lf, tc=SAMPLE_CUMSUM_TILE, d=d, emit_bias_lanes=False)
        return _sample_attn(proj["q"], proj["k"], proj["v"], cb_k, cb_v, i,
                            f_col_new=f_col[:, past:past + t_new], f_row=f_row)

    sample_out = _trunk(x_sample, mods_s, weights, a_sample, b_sample, bb=bs, ts=t_new,
                              keep=t_new, v_name="v")

    return (prompt_out[0], sample_out[0]) + tuple(prompt_out[1:]) + tuple(sample_out[1:])
```

```python
import functools

import jax
import jax.numpy as jnp
from jax import lax
from jax.experimental import pallas as pl
from jax.experimental.pallas import tpu as pltpu

F32 = jnp.float32
BF16 = jnp.bfloat16

N_HEADS = 16
HEAD_DIM = 64
CHUNK = 64
N_PAST_CHUNKS = 8
A_PAST = N_PAST_CHUNKS * CHUNK
MAX_REL = 256
RMS_EPS = 1e-6
NEG_INF = -1e30
ATTN_SCALE = HEAD_DIM ** -0.5

LANES = 128
HEADS_PER_BLOCK = LANES // HEAD_DIM
VMEM_LIMIT_BYTES = 48 * 1024 * 1024

ROW_TILE = 512
A_TILE = 256
A_WIN = A_TILE + A_PAST
FOX_TILE = 512
FOX_QUERY_SPLIT = 2
FF_CHUNK = 256
REL_PAD = 384
G_WIDTH = 1024
SAMPLE_CUMSUM_TILE = 384


def _params(*sem):
    return pltpu.CompilerParams(dimension_semantics=sem, vmem_limit_bytes=VMEM_LIMIT_BYTES)


def _resident(shape):
    zeros = (0,) * len(shape)
    return pl.BlockSpec(shape, lambda *_: zeros, pipeline_mode=pl.Buffered(1))


def _nt_dot(a, b):
    return lax.dot_general(a, b, (((1,), (1,)), ((), ())), preferred_element_type=F32)


def _modulated_norm(x, g, scale, shift):
    ms = jnp.mean(x * x, axis=-1, keepdims=True)
    y = x * lax.rsqrt(ms + RMS_EPS) * g
    return y * (1.0 + scale) + shift


def _split3(x):
    hi = x.astype(BF16)
    r1 = x - hi.astype(F32)
    mid = r1.astype(BF16)
    lo = (r1 - mid.astype(F32)).astype(BF16)
    return hi, mid, lo


def _lower_lanes(shape):
    return lax.broadcasted_iota(jnp.int32, shape, len(shape) - 1) < HEAD_DIM


def _mod_body(c_ref, w_ref, b_ref, o_ref):
    cs = jax.nn.silu(c_ref[...]).astype(BF16)
    o_ref[0] = jnp.dot(cs, w_ref[0].astype(BF16), preferred_element_type=F32) + b_ref[0]


def _modulation(c, w_mod, b_mod):
    depth, d, n = w_mod.shape
    rows = c.shape[0]
    tn = 1536
    assert n % tn == 0 and rows % 16 == 0
    return pl.pallas_call(
        _mod_body,
        grid=(depth, n // tn),
        in_specs=[pl.BlockSpec((rows, d), lambda l, j: (0, 0)),
                  pl.BlockSpec((1, d, tn), lambda l, j: (l, 0, j)),
                  pl.BlockSpec((1, 1, tn), lambda l, j: (l, 0, j))],
        out_specs=pl.BlockSpec((1, rows, tn), lambda l, j: (l, 0, j)),
        out_shape=jax.ShapeDtypeStruct((depth, rows, n), F32),
        compiler_params=_params("parallel", "parallel"),
        name="modulation",
    )(c, w_mod, b_mod.reshape(depth, 1, n))


def _qkv_body(*refs, d, emit, n_prev):
    x_ref, mod_ref, g_ref, w_ref = refs[:4]
    pos = 4
    if "logf" in emit:
        wf_ref, bf_ref = refs[4:6]
        pos = 6
    prev = dict(zip(("k32t", "v32t"), refs[pos:pos + (2 if n_prev else 0)]))
    out = dict(zip(emit, refs[pos + len(prev):]))
    for name, prev_ref in prev.items():
        out[name][0:n_prev] = prev_ref[...]
    x = x_ref[...]
    bb, ts, _ = x.shape
    h = _modulated_norm(x, g_ref[...], mod_ref[:, 1:2, :], mod_ref[:, 0:1, :])
    hb = h.reshape(bb * ts, d).astype(BF16)

    def put(name, val, dtype):
        out[name][...] = val.reshape(bb, ts, val.shape[-1]).astype(dtype)

    if "q" in emit or "qt" in emit:
        q = jnp.dot(hb, w_ref[:, 0:d], preferred_element_type=F32) * ATTN_SCALE
    if "q" in emit:
        put("q", q, BF16)
    if "qt" in emit:
        out["qt"][0] = q.T.astype(BF16)
    k = jnp.dot(hb, w_ref[:, d:2 * d], preferred_element_type=F32)
    v = jnp.dot(hb, w_ref[:, 2 * d:3 * d], preferred_element_type=F32)
    if "k" in emit:
        put("k", k, BF16)
    if "v" in emit:
        put("v", v, BF16)
    if "vt" in emit or "v32t" in emit:
        v_t = v.T
    if "vt" in emit:
        out["vt"][0] = v_t.astype(BF16)
    if "v32t" in emit:
        out["v32t"][n_prev, 0] = v_t
    if "k32t" in emit:
        out["k32t"][n_prev, 0] = k.T
    if "k32" in emit:
        put("k32", k, F32)
    if "v32" in emit:
        put("v32", v, F32)
    if "logf" in emit:
        z = jnp.dot(hb, wf_ref[...], preferred_element_type=F32) + bf_ref[...]
        put("logf", jax.nn.log_sigmoid(z), F32)


def _qkv_proj(x, mod, g, w_qkv, wf=None, bf=None, *, bb, ts, emit, row_start=0, rows=None,
              state=None):
    b, s, d = x.shape
    rows = s if rows is None else rows
    assert rows % ts == 0 and row_start % ts == 0 and b % bb == 0
    assert ts % 16 == 0 or bb == 1
    assert bb == 1 or not {"qt", "vt", "k32t", "v32t"} & set(emit)
    off = row_start // ts
    n_prev = 0 if state is None else state["k32t"].shape[0]
    assert ("logf" in emit) == (wf is not None)
    row_spec = lambda width: pl.BlockSpec((bb, ts, width), lambda i, j: (i, j, 0))
    in_specs = [pl.BlockSpec((bb, ts, d), lambda i, j: (i, j + off, 0)),
                pl.BlockSpec((bb, 6, d), lambda i, j: (i, 0, 0)),
                _resident((1, d)),
                _resident((d, 3 * d))]
    args = [x, mod, g.reshape(1, d), w_qkv]
    if wf is not None:
        in_specs += [_resident((d, LANES)), _resident((1, LANES))]
        args += [wf, bf]
    kinds = {
        "q": (row_spec(d), (b, rows, d), BF16), "k": (row_spec(d), (b, rows, d), BF16),
        "v": (row_spec(d), (b, rows, d), BF16),
        "vt": (pl.BlockSpec((1, d, ts), lambda i, j: (i, 0, j)), (b, d, rows), BF16),
        "qt": (pl.BlockSpec((1, d, ts), lambda i, j: (i, 0, j)), (b, d, rows), BF16),
        "k32": (row_spec(d), (b, rows, d), F32), "v32": (row_spec(d), (b, rows, d), F32),
        "logf": (row_spec(LANES), (b, rows, LANES), F32),
    }
    stack_spec = lambda n: pl.BlockSpec((n, 1, d, ts), lambda i, j: (0, i, 0, j))
    kinds["k32t"] = kinds["v32t"] = (stack_spec(n_prev + 1), (n_prev + 1, b, d, rows), F32)
    if n_prev:
        in_specs += [stack_spec(n_prev)] * 2
        args += [state["k32t"], state["v32t"]]
    outs = pl.pallas_call(
        functools.partial(_qkv_body, d=d, emit=emit, n_prev=n_prev),
        grid=(b // bb, rows // ts),
        in_specs=in_specs,
        out_specs=[kinds[e][0] for e in emit],
        out_shape=[jax.ShapeDtypeStruct(kinds[e][1], kinds[e][2]) for e in emit],
        compiler_params=_params("parallel", "parallel"),
        name="norm_qkv",
    )(*args)
    return dict(zip(emit, outs))


def _post_attn_body(*refs, d_ff, final_norm):
    o_ref, x_ref, mod_ref, wo_ref, g_ref, wgu_ref, wd_ref = refs[:7]
    out_ref = refs[-1]
    x = x_ref[...]
    bb, ts, d = x.shape
    mod = lambda r: mod_ref[:, r:r + 1, :]
    y = jnp.dot(o_ref[...].reshape(bb * ts, d), wo_ref[...], preferred_element_type=F32)
    x1 = x + mod(2) * y.reshape(bb, ts, d)
    h = _modulated_norm(x1, g_ref[...], mod(4), mod(3))
    hb = h.reshape(bb * ts, d).astype(BF16)
    acc = jnp.zeros((bb * ts, d), F32)
    for c in range(d_ff // FF_CHUNK):
        lo = c * FF_CHUNK
        gate = jnp.dot(hb, wgu_ref[:, lo:lo + FF_CHUNK], preferred_element_type=F32)
        up = jnp.dot(hb, wgu_ref[:, d_ff + lo:d_ff + lo + FF_CHUNK], preferred_element_type=F32)
        act = (jax.nn.silu(gate) * up).astype(BF16)
        acc = acc + jnp.dot(act, wd_ref[lo:lo + FF_CHUNK, :], preferred_element_type=F32)
    x2 = x1 + mod(5) * acc.reshape(bb, ts, d)
    if final_norm:
        gout_ref = refs[7]
        ms = jnp.mean(x2 * x2, axis=-1, keepdims=True)
        x2 = x2 * lax.rsqrt(ms + RMS_EPS) * gout_ref[...]
    out_ref[...] = x2


def _post_attn(o, x, mod, w_o, g, w_gu, w_down, g_out=None, *, bb, ts):
    b, s, d = x.shape
    d_ff = w_down.shape[0]
    assert s % ts == 0 and b % bb == 0 and d_ff % FF_CHUNK == 0
    assert ts % 16 == 0 or bb == 1
    row = pl.BlockSpec((bb, ts, d), lambda i, j: (i, j, 0))
    in_specs = [row, row, pl.BlockSpec((bb, 6, d), lambda i, j: (i, 0, 0)), _resident((d, d)),
                _resident((1, d)), _resident((d, 2 * d_ff)), _resident((d_ff, d))]
    args = [o, x, mod, w_o, g.reshape(1, d), w_gu, w_down]
    if g_out is not None:
        in_specs.append(_resident((1, d)))
        args.append(g_out.reshape(1, d))
    return pl.pallas_call(
        functools.partial(_post_attn_body, d_ff=d_ff, final_norm=g_out is not None),
        grid=(b // bb, s // ts),
        in_specs=in_specs, out_specs=row,
        out_shape=jax.ShapeDtypeStruct((b, s, d), F32),
        compiler_params=_params("parallel", "parallel"),
        name="oproj_ffn",
    )(*args)


def _bias_row_body(tab_ref, g_ref):
    u = lax.broadcasted_iota(jnp.int32, (REL_PAD, G_WIDTH), 1)
    r = lax.broadcasted_iota(jnp.int32, (REL_PAD, G_WIDTH), 0)
    rel = A_PAST + (A_TILE - 1) - u
    idx = jnp.clip(rel, -(CHUNK - 1), MAX_REL) + (CHUNK - 1)
    onehot = jnp.where(r == idx, 1.0, 0.0).astype(BF16)
    hi, mid, lo = _split3(tab_ref[...])
    g_ref[...] = (jnp.dot(hi, onehot, preferred_element_type=F32)
                  + jnp.dot(mid, onehot, preferred_element_type=F32)
                  + jnp.dot(lo, onehot, preferred_element_type=F32))


def _bias_tile_body(g_ref, o_ref, ot_ref):
    x = jnp.broadcast_to(g_ref[0], (A_TILE, G_WIDTH))
    rolled = pltpu.roll(x, G_WIDTH - (A_TILE - 1), 1, stride=1, stride_axis=0)
    val = rolled[:, :A_WIN]
    qc = lax.broadcasted_iota(jnp.int32, (A_TILE, A_WIN), 0) // CHUNK
    kc = lax.broadcasted_iota(jnp.int32, (A_TILE, A_WIN), 1) // CHUNK
    band = (kc >= qc) & (kc <= qc + N_PAST_CHUNKS)
    tile = jnp.where(band, val, NEG_INF)
    o_ref[0] = tile
    ot_ref[0] = tile.T


def _rel_bias_tile(table):
    h, rel = table.shape
    assert rel == CHUNK + MAX_REL and rel <= REL_PAD and A_WIN + A_TILE - 1 <= G_WIDTH
    tab = jnp.pad(table, ((0, 0), (0, REL_PAD - rel)))
    gen = pl.pallas_call(
        _bias_row_body,
        out_shape=jax.ShapeDtypeStruct((h, G_WIDTH), F32),
        name="rel_bias_row",
    )(tab)
    return pl.pallas_call(
        _bias_tile_body,
        grid=(h,),
        in_specs=[pl.BlockSpec((1, 1, G_WIDTH), lambda i: (i, 0, 0))],
        out_specs=[pl.BlockSpec((1, A_TILE, A_WIN), lambda i: (i, 0, 0)),
                   pl.BlockSpec((1, A_WIN, A_TILE), lambda i: (i, 0, 0))],
        out_shape=[jax.ShapeDtypeStruct((h, A_TILE, A_WIN), F32),
                   jax.ShapeDtypeStruct((h, A_WIN, A_TILE), F32)],
        compiler_params=_params("parallel"),
        name="rel_bias_tile",
    )(gen.reshape(h, 1, G_WIDTH))


def _head_select(shape, hh):
    lower = _lower_lanes(shape)
    return lower if hh == 0 else jnp.logical_not(lower)


def _chunk_attn_body(q_ref, k_ref, vt_ref, bias_ref, o_ref, s0_ref, s1_ref, m0_ref, m1_ref,
                     *, n_tiles):
    sets = ((s0_ref, m0_ref), (s1_ref, m1_ref))

    def head_queries(qt):
        zero = jnp.zeros((HEAD_DIM, qt.shape[1]), qt.dtype)
        return (jnp.concatenate([qt[:HEAD_DIM], zero], axis=0),
                jnp.concatenate([zero, qt[HEAD_DIM:]], axis=0))

    def head_values(vt):
        ones = jnp.ones((HEAD_DIM, vt.shape[1]), BF16)
        return (jnp.concatenate([vt[:HEAD_DIM], ones], axis=0),
                jnp.concatenate([ones, vt[HEAD_DIM:]], axis=0))

    def finish(accs):
        o0 = accs[0][:HEAD_DIM] / accs[0][HEAD_DIM:]
        o1 = accs[1][HEAD_DIM:] / accs[1][:HEAD_DIM]
        return jnp.concatenate([o0, o1], axis=0).T.astype(o_ref.dtype)

    def edge_tile(t):
        nk = (t + 1) * A_TILE
        qm = head_queries(q_ref[0, :, t * A_TILE:(t + 1) * A_TILE])
        kw = k_ref[0, 0:nk, :]
        vh = head_values(vt_ref[0, :, 0:nk])
        accs = []
        for hh in range(HEADS_PER_BLOCK):
            s = (jnp.dot(kw, qm[hh], preferred_element_type=F32)
                 + bias_ref[hh, A_WIN - nk:A_WIN, :])
            pt = jnp.exp(s - jnp.max(s, axis=0, keepdims=True)).astype(BF16)
            accs.append(jnp.dot(vh[hh], pt, preferred_element_type=F32))
        o_ref[0, t * A_TILE:(t + 1) * A_TILE, :] = finish(accs)

    def score_step(t, dst):
        s_ref, m_ref = sets[dst]
        qm = head_queries(q_ref[0, :, pl.ds(pl.multiple_of(t * A_TILE, A_TILE), A_TILE)])
        kw = k_ref[0, pl.ds(pl.multiple_of((t - 2) * A_TILE, A_TILE), A_WIN), :]
        for hh in range(HEADS_PER_BLOCK):
            s = jnp.dot(kw, qm[hh], preferred_element_type=F32) + bias_ref[hh]
            s_ref[hh] = s
            m_ref[hh] = jnp.max(s, axis=0, keepdims=True)

    def value_step(t, src):
        s_ref, m_ref = sets[src]
        vh = head_values(vt_ref[0, :, pl.ds(pl.multiple_of((t - 2) * A_TILE, A_TILE), A_WIN)])
        accs = []
        for hh in range(HEADS_PER_BLOCK):
            pt = jnp.exp(s_ref[hh] - m_ref[hh]).astype(BF16)
            accs.append(jnp.dot(vh[hh], pt, preferred_element_type=F32))
        o_ref[0, pl.ds(pl.multiple_of(t * A_TILE, A_TILE), A_TILE), :] = finish(accs)

    edge_tile(0)
    edge_tile(1)
    score_step(2, 0)

    def two_tiles(u):
        t = 2 + 2 * u
        score_step(t + 1, 1)
        value_step(t, 0)
        score_step(t + 2, 0)
        value_step(t + 1, 1)

    def four_tiles(w, carry):
        two_tiles(2 * w)
        two_tiles(2 * w + 1)
        return carry

    pairs_of_tiles = (n_tiles - 4) // 2
    lax.fori_loop(0, pairs_of_tiles // 2, four_tiles, 0)
    if pairs_of_tiles % 2:
        two_tiles(pairs_of_tiles - 1)
    score_step(n_tiles - 1, 1)
    value_step(n_tiles - 2, 0)
    value_step(n_tiles - 1, 1)


def _chunk_attn_prompt(qt, k, vt, bias_t):
    b, s, d = k.shape
    n_tiles = s // A_TILE
    assert A_WIN == 3 * A_TILE and s % A_TILE == 0 and d % LANES == 0
    assert n_tiles >= 4 and n_tiles % 2 == 0
    pairs = d // LANES
    rows = pl.BlockSpec((1, s, LANES), lambda p, i: (i, 0, p))
    cols = pl.BlockSpec((1, LANES, s), lambda p, i: (i, p, 0))
    return pl.pallas_call(
        functools.partial(_chunk_attn_body, n_tiles=n_tiles),
        grid=(pairs, b),
        in_specs=[cols, rows, cols,
                  pl.BlockSpec((HEADS_PER_BLOCK, A_WIN, A_TILE), lambda p, i: (p, 0, 0))],
        out_specs=rows,
        out_shape=jax.ShapeDtypeStruct((b, s, d), BF16),
        scratch_shapes=[pltpu.VMEM((HEADS_PER_BLOCK, A_WIN, A_TILE), F32)] * 2
        + [pltpu.VMEM((HEADS_PER_BLOCK, 1, A_TILE), F32)] * 2,
        compiler_params=_params("parallel", "parallel"),
        name="chunk_attn",
    )(qt, k, vt, bias_t)


def _sample_attn_body(*refs, n_cache, forget):
    q_ref, kc_ref, vc_ref, kn_ref, vn_ref = refs[:5]
    o_ref = refs[-1]
    t, d = q_ref.shape[1:]
    if forget:
        fq_ref, fk_ref = refs[5:7]
        causal = (lax.broadcasted_iota(jnp.int32, (t, t), 1)
                  <= lax.broadcasted_iota(jnp.int32, (t, t), 0))
    else:
        bias_ref = refs[5]
    for p in range(d // LANES):
        lanes = slice(p * LANES, (p + 1) * LANES)
        q = q_ref[0, :, lanes]
        kc_t = kc_ref[0, 0, lanes, :].astype(BF16)
        vc_t = vc_ref[0, 0, lanes, :].astype(BF16)
        kn = kn_ref[0, :, lanes]
        vn = vn_ref[0, :, lanes]
        outs = []
        for hh in range(HEADS_PER_BLOCK):
            head = HEADS_PER_BLOCK * p + hh
            qm = jnp.where(_head_select(q.shape, hh), q, jnp.zeros_like(q))
            if forget:
                fq = fq_ref[0, :, head:head + 1]
                sc = (jnp.dot(qm, kc_t, preferred_element_type=F32)
                      + (fq - fk_ref[0, head:head + 1, 0:n_cache]))
                sn = _nt_dot(qm, kn) + (fq - fk_ref[0, head:head + 1, n_cache:n_cache + t])
                sn = jnp.where(causal, sn, NEG_INF)
            else:
                sc = (jnp.dot(qm, kc_t, preferred_element_type=F32)
                      + bias_ref[head, :, 0:n_cache])
                sn = _nt_dot(qm, kn) + bias_ref[head, :, n_cache:n_cache + t]
            m = jnp.maximum(jnp.max(sc, axis=-1, keepdims=True),
                            jnp.max(sn, axis=-1, keepdims=True))
            pc = jnp.exp(sc - m)
            pn = jnp.exp(sn - m)
            l = jnp.sum(pc, axis=-1, keepdims=True) + jnp.sum(pn, axis=-1, keepdims=True)
            o = (_nt_dot(pc.astype(BF16), vc_t)
                 + jnp.dot(pn.astype(BF16), vn, preferred_element_type=F32))
            outs.append(o / l)
        o_ref[0, :, lanes] = jnp.where(_lower_lanes(outs[0].shape), outs[0],
                                       outs[1]).astype(o_ref.dtype)


def _sample_attn(q, k, v, k_cache, v_cache, layer, *, bias=None, f_col_new=None, f_row=None):
    b, t, d = q.shape
    n_cache = k_cache.shape[3]
    new = pl.BlockSpec((1, t, d), lambda i: (i, 0, 0))
    cache = pl.BlockSpec((1, 1, d, n_cache), lambda i: (layer, i, 0, 0))
    forget = bias is None
    if forget:
        extra_specs = [pl.BlockSpec((1, t, LANES), lambda i: (i, 0, 0)),
                       pl.BlockSpec((1,) + f_row.shape[1:], lambda i: (i, 0, 0))]
        extra = [f_col_new, f_row]
    else:
        assert n_cache == A_PAST and t <= CHUNK and n_cache + t <= A_WIN
        extra_specs = [pl.BlockSpec((bias.shape[0], t, A_WIN), lambda i: (0, 0, 0))]
        extra = [bias]
    return pl.pallas_call(
        functools.partial(_sample_attn_body, n_cache=n_cache, forget=forget),
        grid=(b,),
        in_specs=[new, cache, cache, new, new] + extra_specs,
        out_specs=new,
        out_shape=jax.ShapeDtypeStruct((b, t, d), BF16),
        compiler_params=_params("parallel"),
        name="sample_attn",
    )(q, k_cache, v_cache, k, v, *extra)


def _bias_lane_selector(first, d):
    col = lax.broadcasted_iota(jnp.int32, (3 * LANES, d), 1)
    row = lax.broadcasted_iota(jnp.int32, (3 * LANES, d), 0)
    lane = col % LANES
    head = HEADS_PER_BLOCK * (col // LANES) + jnp.where(lane >= HEAD_DIM, 0, 1)
    hit = ((row % LANES) == head) & ((lane % HEAD_DIM) == first + row // LANES)
    return jnp.where(hit, 1.0, 0.0).astype(BF16)


def _bias_lanes(f, sel_q, sel_k, d):
    pieces = jnp.concatenate(_split3(f), axis=1)
    slot_row = lax.broadcasted_iota(jnp.int32, (1, d), 1) % HEAD_DIM
    ones = lambda first: jnp.where((slot_row >= first) & (slot_row < first + 3), 1.0, 0.0)
    q_side = jnp.dot(pieces, sel_q, preferred_element_type=F32) + ones(3)
    k_side = ones(0) - jnp.dot(pieces, sel_k, preferred_element_type=F32)
    return q_side, k_side


def _cumsum_body(x_ref, *refs, tc, n_heads, d, emit_bias_lanes):
    out_a_ref, out_b_ref, carry_ref, tri_ref = refs[:4]

    @pl.when(pl.program_id(1) == 0)
    def _():
        carry_ref[...] = jnp.zeros_like(carry_ref)
        r = lax.broadcasted_iota(jnp.int32, (tc, tc), 0)
        c = lax.broadcasted_iota(jnp.int32, (tc, tc), 1)
        tri_ref[...] = jnp.where(c <= r, 1.0, 0.0).astype(BF16)
        if emit_bias_lanes:
            refs[4][...] = _bias_lane_selector(0, d)
            refs[5][...] = _bias_lane_selector(3, d)

    pieces = jnp.concatenate(_split3(x_ref[0]), axis=1)
    sums = jnp.dot(tri_ref[...], pieces, preferred_element_type=F32)
    f = ((sums[:, 0:LANES] + sums[:, LANES:2 * LANES]) + sums[:, 2 * LANES:3 * LANES]
         + carry_ref[0:1, :])
    carry_ref[0:1, :] = f[tc - 1:tc, :]
    if emit_bias_lanes:
        q_side, k_side = _bias_lanes(f, refs[4][...], refs[5][...], d)
        out_a_ref[0] = q_side.T.astype(BF16)
        out_b_ref[0] = k_side.astype(BF16)
    else:
        out_a_ref[0] = f
        out_b_ref[0] = f.T[0:n_heads, :]


def _cumsum(logf, *, tc, d, emit_bias_lanes, n_heads=N_HEADS):
    b, s, w = logf.shape
    assert s % tc == 0 and tc % LANES == 0 and w == LANES
    scratch = [pltpu.VMEM((8, LANES), F32), pltpu.VMEM((tc, tc), BF16)]
    if emit_bias_lanes:
        out_specs = [pl.BlockSpec((1, d, tc), lambda i, j: (i, 0, j)),
                     pl.BlockSpec((1, tc, d), lambda i, j: (i, j, 0))]
        out_shape = [jax.ShapeDtypeStruct((b, d, s), BF16), jax.ShapeDtypeStruct((b, s, d), BF16)]
        scratch += [pltpu.VMEM((3 * LANES, d), BF16)] * 2
    else:
        out_specs = [pl.BlockSpec((1, tc, LANES), lambda i, j: (i, j, 0)),
                     pl.BlockSpec((1, n_heads, tc), lambda i, j: (i, 0, j))]
        out_shape = [jax.ShapeDtypeStruct((b, s, LANES), F32),
                     jax.ShapeDtypeStruct((b, n_heads, s), F32)]
    return pl.pallas_call(
        functools.partial(_cumsum_body, tc=tc, n_heads=n_heads, d=d,
                          emit_bias_lanes=emit_bias_lanes),
        grid=(b, s // tc),
        in_specs=[pl.BlockSpec((1, tc, LANES), lambda i, j: (i, j, 0))],
        out_specs=out_specs, out_shape=out_shape,
        scratch_shapes=scratch,
        compiler_params=_params("parallel", "arbitrary"),
        name="logf_cumsum",
    )(logf)


def _fox_body(q_ref, qa_ref, k_ref, ka_ref, vt_ref, o_ref, qh_ref, acc_ref,
              s_ref, m_ref, a_ref, *, tile, n_blocks):
    qw = tile // FOX_QUERY_SPLIT
    chains = [(hh, qs) for hh in range(HEADS_PER_BLOCK) for qs in range(FOX_QUERY_SPLIT)]
    lower = _lower_lanes((tile, LANES))
    ones = jnp.ones((HEAD_DIM, tile), BF16)
    block_rows = lambda i: pl.ds(pl.multiple_of(i * tile, tile), tile)

    def load_queries(i):
        qt = q_ref[0, :, block_rows(i)]
        qat = qa_ref[0, :, block_rows(i)]
        qh_ref[0] = jnp.concatenate([qt[:HEAD_DIM], qat[HEAD_DIM:]], axis=0)
        qh_ref[1] = jnp.concatenate([qat[:HEAD_DIM], qt[HEAD_DIM:]], axis=0)

    def score_step(j, dst, src, diagonal=False):
        off = pl.multiple_of(j * tile, tile)
        kb = k_ref[0, pl.ds(off, tile), :]
        ka = ka_ref[0, pl.ds(off, tile), :]
        kh = (jnp.where(lower, kb, ka), jnp.where(lower, ka, kb))
        for c, (hh, qs) in enumerate(chains):
            s = jnp.dot(kh[hh], qh_ref[hh, :, qs * qw:(qs + 1) * qw],
                        preferred_element_type=F32)
            if diagonal:
                key = lax.broadcasted_iota(jnp.int32, s.shape, 0)
                qry = lax.broadcasted_iota(jnp.int32, s.shape, 1) + qs * qw
                s = jnp.where(key <= qry, s, NEG_INF)
            m_prev = jnp.full((1, qw), NEG_INF, F32) if src is None else m_ref[src, c]
            m_new = jnp.maximum(m_prev, jnp.max(s, axis=0, keepdims=True))
            s_ref[dst, c] = s
            m_ref[dst, c] = m_new
            a_ref[dst, c] = jnp.exp(m_prev - m_new)

    def value_step(j, src, acc):
        off = pl.multiple_of(j * tile, tile)
        vt = vt_ref[0, :, pl.ds(off, tile)]
        vh = (jnp.concatenate([vt[:HEAD_DIM], ones], axis=0),
              jnp.concatenate([ones, vt[HEAD_DIM:]], axis=0))
        for c, (hh, qs) in enumerate(chains):
            pt = jnp.exp(s_ref[src, c] - m_ref[src, c]).astype(BF16)
            cols = slice(qs * qw, (qs + 1) * qw)
            acc_ref[acc, hh, :, cols] = (a_ref[src, c] * acc_ref[acc, hh, :, cols]
                                         + jnp.dot(vh[hh], pt, preferred_element_type=F32))

    def finish(i, acc):
        o0 = acc_ref[acc, 0, :HEAD_DIM, :] / acc_ref[acc, 0, HEAD_DIM:, :]
        o1 = acc_ref[acc, 1, HEAD_DIM:, :] / acc_ref[acc, 1, :HEAD_DIM, :]
        o_ref[0, block_rows(i), :] = jnp.concatenate([o0, o1], axis=0).T.astype(o_ref.dtype)

    def steady(i, acc, first, second, count):
        block_of_step = lambda t: jnp.where(t == 0, i, t - 1)

        def two_steps(pp):
            t = 2 * pp
            score_step(t, second, first)
            value_step(block_of_step(t), first, acc)
            score_step(t + 1, first, second)
            value_step(t, second, acc)

        def four_steps(u, carry):
            two_steps(2 * u)
            two_steps(2 * u + 1)
            return carry

        lax.fori_loop(0, count // 2, four_steps, 0)

        @pl.when(count % 2 == 1)
        def _():
            two_steps(count - 1)

    def pair_of_blocks(r, carry):
        even, odd = 2 * r, 2 * r + 1
        load_queries(even)
        acc_ref[0] = jnp.zeros(acc_ref.shape[1:], F32)
        score_step(even, 0, None, diagonal=True)
        value_step(jnp.maximum(even - 2, 0), 2, 1)
        finish(jnp.maximum(even - 1, 0), 1)
        steady(even, 0, 0, 1, r)
        load_queries(odd)
        acc_ref[1] = jnp.zeros(acc_ref.shape[1:], F32)
        score_step(odd, 1, None, diagonal=True)
        value_step(jnp.where(even == 0, even, even - 1), 0, 0)
        finish(even, 0)
        steady(odd, 1, 1, 2, r)
        score_step(odd - 1, 2, 1)
        value_step(jnp.where(odd == 1, odd, odd - 2), 1, 1)
        return carry

    s_ref[2] = jnp.zeros(s_ref.shape[1:], F32)
    m_ref[2] = jnp.zeros(m_ref.shape[1:], F32)
    a_ref[2] = jnp.zeros(a_ref.shape[1:], F32)
    acc_ref[1] = jnp.zeros(acc_ref.shape[1:], F32)
    lax.fori_loop(0, n_blocks // 2, pair_of_blocks, 0)
    value_step(n_blocks - 2, 2, 1)
    finish(n_blocks - 1, 1)


def _fox_attn_prompt(qt, qat, k, ka, vt):
    b, s, d = k.shape
    tile = FOX_TILE
    assert s % (2 * tile) == 0
    pairs = d // LANES
    n_chains = HEADS_PER_BLOCK * FOX_QUERY_SPLIT
    qw = tile // FOX_QUERY_SPLIT
    seq = pl.BlockSpec((1, s, LANES), lambda i, p: (i, 0, p))
    seq_t = pl.BlockSpec((1, LANES, s), lambda i, p: (i, p, 0))
    return pl.pallas_call(
        functools.partial(_fox_body, tile=tile, n_blocks=s // tile),
        grid=(b, pairs),
        in_specs=[seq_t, seq_t, seq, seq, seq_t],
        out_specs=seq,
        out_shape=jax.ShapeDtypeStruct((b, s, d), BF16),
        scratch_shapes=[pltpu.VMEM((HEADS_PER_BLOCK, LANES, tile), BF16),
                        pltpu.VMEM((2, HEADS_PER_BLOCK, LANES, tile), F32),
                        pltpu.VMEM((3, n_chains, tile, qw), F32),
                        pltpu.VMEM((3, n_chains, 1, qw), F32),
                        pltpu.VMEM((3, n_chains, 1, qw), F32)],
        compiler_params=_params("parallel", "parallel"),
        name="fox_attn",
    )(qt, qat, k, ka, vt)


def _trunk(x, mods, weights, a_mix, b_mix, *, bb, ts, keep, v_name):
    b, s, d = x.shape
    depth = mods.shape[0]
    transposed = v_name == "vt"
    k_name, v32_name = ("k32t", "v32t") if transposed else ("k32", "v32")
    q_name = "qt" if transposed else "q"
    a_state, b_state, b_logf = None, None, []
    stacked = lambda st: {k_name: jnp.stack([e[k_name] for e in st]),
                          v32_name: jnp.stack([e[v32_name] for e in st])}
    a_list, b_list = [], []
    for l in range(depth):
        mod = mods[l]
        g, w = weights["g_mix"][l], weights["w_qkv"][l]
        if l % 2 == 0:
            kw = dict(state=a_state) if transposed else {}
            if keep == s:
                proj = kv32 = _qkv_proj(x, mod, g, w, bb=bb, ts=ts,
                                        emit=(q_name, "k", v_name, k_name, v32_name), **kw)
            else:
                proj = _qkv_proj(x, mod, g, w, bb=bb, ts=ts, emit=(q_name, "k", v_name))
                kv32 = _qkv_proj(x, mod, g, w, bb=bb, ts=min(ts, keep), emit=(k_name, v32_name),
                                 row_start=s - keep, rows=keep, **kw)
            a_state = kv32
            a_list.append(kv32)
            o = a_mix(l // 2, proj)
        else:
            ib = l // 2
            kw = dict(state=b_state) if transposed else {}
            ts_b = ts // 2 if transposed and b_state is not None else ts
            proj = _qkv_proj(x, mod, g, w, weights["w_fgate"][ib], weights["b_fgate"][ib],
                             bb=bb, ts=ts_b, emit=(q_name, "k", v_name, k_name, v32_name, "logf"),
                             **kw)
            b_state = proj
            b_list.append(proj)
            b_logf.append(proj["logf"][:, :, :N_HEADS])
            o = b_mix(ib, proj)
        g_out = weights["g_out"] if l == depth - 1 else None
        x = _post_attn(o, x, mod, weights["w_o"][l], weights["g_ffn"][l], weights["w_gu"][l],
                       weights["w_down"][l], g_out, bb=bb, ts=ts)
    if not transposed:
        a_state, b_state = stacked(a_list), stacked(b_list)

    def heads(a):
        if transposed:
            n, bsz, _, r = a.shape
            return a.reshape(n, bsz, N_HEADS, HEAD_DIM, r).transpose(0, 1, 4, 2, 3)
        return a.reshape(a.shape[:3] + (N_HEADS, HEAD_DIM))

    return (x, heads(a_state[k_name]), heads(a_state[v32_name]),
            heads(b_state[k_name]), heads(b_state[v32_name]), jnp.stack(b_logf))


def kernel(x_prompt, x_sample, cache_a_k, cache_a_v, cache_b_k, cache_b_v, cache_b_logf,
           c_prompt, c_sample, w_mod, b_mod, g_mix, g_ffn, w_qkv, w_o, rel_bias,
           w_fgate, b_fgate, w_gu, w_down, g_out):
    bp, sp, d = x_prompt.shape
    bs, t_new, _ = x_sample.shape
    depth = w_mod.shape[0]
    n_b = w_fgate.shape[0]
    assert d == N_HEADS * HEAD_DIM
    keep = min(A_PAST, sp)

    weights = {
        "g_mix": g_mix, "g_ffn": g_ffn, "g_out": g_out,
        "w_qkv": w_qkv.astype(BF16), "w_o": w_o.astype(BF16),
        "w_gu": w_gu.astype(BF16), "w_down": w_down.astype(BF16),
        "w_fgate": jnp.pad(w_fgate, ((0, 0), (0, 0), (0, LANES - N_HEADS))).astype(BF16),
        "b_fgate": jnp.pad(b_fgate, ((0, 0), (0, LANES - N_HEADS))).reshape(n_b, 1, LANES),
    }

    c_all = jnp.concatenate([c_prompt, c_sample], axis=0)
    rows = -(-c_all.shape[0] // 16) * 16
    c_all = jnp.pad(c_all, ((0, rows - c_all.shape[0]), (0, 0)))
    mods = _modulation(c_all, w_mod, b_mod)
    mods_p = mods[:, :bp].reshape(depth, bp, 6, d)
    mods_s = mods[:, bp:bp + bs].reshape(depth, bs, 6, d)

    bias = [_rel_bias_tile(rel_bias[i]) for i in range(rel_bias.shape[0])]

    def a_prompt(i, proj):
        return _chunk_attn_prompt(proj["qt"], proj["k"], proj["vt"], bias[i][1])

    def b_prompt(i, proj):
        qa, ka = _cumsum(proj["logf"], tc=FOX_TILE, d=d, emit_bias_lanes=True)
        return _fox_attn_prompt(proj["qt"], qa, proj["k"], ka, proj["vt"])

    prompt_out = _trunk(x_prompt, mods_p, weights, a_prompt, b_prompt, bb=1, ts=ROW_TILE,
                              keep=keep, v_name="vt")

    by_feature = lambda c: c.transpose(0, 1, 3, 4, 2).reshape(c.shape[0], bs, d, c.shape[2])
    ca_k, ca_v = by_feature(cache_a_k), by_feature(cache_a_v)
    cb_k, cb_v = by_feature(cache_b_k), by_feature(cache_b_v)
    past = cache_b_k.shape[2]
    total = -(-(past + t_new) // SAMPLE_CUMSUM_TILE) * SAMPLE_CUMSUM_TILE

    def a_sample(i, proj):
        return _sample_attn(proj["q"], proj["k"], proj["v"], ca_k, ca_v, i, bias=bias[i][0])

    def b_sample(i, proj):
        lf_cache = jnp.pad(cache_b_logf[i].astype(F32), ((0, 0), (0, 0), (0, LANES - N_HEADS)))
        lf = jnp.concatenate([lf_cache, proj["logf"]], axis=1)
        lf = jnp.pad(lf, ((0, 0), (0, total - (past + t_new)), (0, 0)))
        f_col, f_row = _cumsum(lf, tc=SAMPLE_CUMSUM_TILE, d=d, emit_bias_lanes=False)
        return _sample_attn(proj["q"], proj["k"], proj["v"], cb_k, cb_v, i,
                            f_col_new=f_col[:, past:past + t_new], f_row=f_row)

    sample_out = _trunk(x_sample, mods_s, weights, a_sample, b_sample, bb=bs, ts=t_new,
                              keep=t_new, v_name="v")

    return (prompt_out[0], sample_out[0]) + tuple(prompt_out[1:]) + tuple(sample_out[1:])
```

```python
import functools

import jax
import jax.numpy as jnp
from jax import lax
from jax.experimental import pallas as pl
from jax.experimental.pallas import tpu as pltpu

F32 = jnp.float32
BF16 = jnp.bfloat16

N_HEADS = 16
HEAD_DIM = 64
CHUNK = 64
N_PAST_CHUNKS = 8
A_PAST = N_PAST_CHUNKS * CHUNK
MAX_REL = 256
RMS_EPS = 1e-6
NEG_INF = -1e30
ATTN_SCALE = HEAD_DIM ** -0.5

LANES = 128
HEADS_PER_BLOCK = LANES // HEAD_DIM
VMEM_LIMIT_BYTES = 48 * 1024 * 1024

ROW_TILE = 512
A_TILE = 256
A_WIN = A_TILE + A_PAST
FOX_TILE = 512
FOX_QUERY_SPLIT = 2
FF_CHUNK = 256
REL_PAD = 384
G_WIDTH = 1024
SAMPLE_CUMSUM_TILE = 384


def _params(*sem):
    return pltpu.CompilerParams(dimension_semantics=sem, vmem_limit_bytes=VMEM_LIMIT_BYTES)


def _resident(shape):
    zeros = (0,) * len(shape)
    return pl.BlockSpec(shape, lambda *_: zeros, pipeline_mode=pl.Buffered(1))


def _nt_dot(a, b):
    return lax.dot_general(a, b, (((1,), (1,)), ((), ())), preferred_element_type=F32)


def _modulated_norm(x, g, scale, shift):
    ms = jnp.mean(x * x, axis=-1, keepdims=True)
    y = x * lax.rsqrt(ms + RMS_EPS) * g
    return y * (1.0 + scale) + shift


def _split3(x):
    hi = x.astype(BF16)
    r1 = x - hi.astype(F32)
    mid = r1.astype(BF16)
    lo = (r1 - mid.astype(F32)).astype(BF16)
    return hi, mid, lo


def _lower_lanes(shape):
    return lax.broadcasted_iota(jnp.int32, shape, len(shape) - 1) < HEAD_DIM


def _mod_body(c_ref, w_ref, b_ref, o_ref):
    cs = jax.nn.silu(c_ref[...]).astype(BF16)
    o_ref[0] = jnp.dot(cs, w_ref[0].astype(BF16), preferred_element_type=F32) + b_ref[0]


def _modulation(c, w_mod, b_mod):
    depth, d, n = w_mod.shape
    rows = c.shape[0]
    tn = 1536
    assert n % tn == 0 and rows % 16 == 0
    return pl.pallas_call(
        _mod_body,
        grid=(depth, n // tn),
        in_specs=[pl.BlockSpec((rows, d), lambda l, j: (0, 0)),
                  pl.BlockSpec((1, d, tn), lambda l, j: (l, 0, j)),
                  pl.BlockSpec((1, 1, tn), lambda l, j: (l, 0, j))],
        out_specs=pl.BlockSpec((1, rows, tn), lambda l, j: (l, 0, j)),
        out_shape=jax.ShapeDtypeStruct((depth, rows, n), F32),
        compiler_params=_params("parallel", "parallel"),
        name="modulation",
    )(c, w_mod, b_mod.reshape(depth, 1, n))


def _qkv_body(*refs, d, emit, n_aliased):
    x_ref, mod_ref, g_ref, w_ref = refs[:4]
    pos = 4
    if "logf" in emit:
        wf_ref, bf_ref = refs[4:6]
        pos = 6
    out = dict(zip(emit, refs[pos + n_aliased:]))
    x = x_ref[...]
    bb, ts, _ = x.shape
    h = _modulated_norm(x, g_ref[...], mod_ref[:, 1:2, :], mod_ref[:, 0:1, :])
    hb = h.reshape(bb * ts, d).astype(BF16)

    def put(name, val, dtype):
        out[name][...] = val.reshape(bb, ts, val.shape[-1]).astype(dtype)

    if "q" in emit or "qt" in emit:
        q = jnp.dot(hb, w_ref[:, 0:d], preferred_element_type=F32) * ATTN_SCALE
    if "q" in emit:
        put("q", q, BF16)
    if "qt" in emit:
        out["qt"][0] = q.T.astype(BF16)
    k = jnp.dot(hb, w_ref[:, d:2 * d], preferred_element_type=F32)
    v = jnp.dot(hb, w_ref[:, 2 * d:3 * d], preferred_element_type=F32)
    if "k" in emit:
        put("k", k, BF16)
    if "v" in emit:
        put("v", v, BF16)
    if "vt" in emit or "v32t" in emit:
        v_t = v.T
    if "vt" in emit:
        out["vt"][0] = v_t.astype(BF16)
    if "v32t" in emit:
        for slot in range(out["v32t"].shape[0]):
            out["v32t"][slot, 0] = v_t
    if "k32t" in emit:
        k_t = k.T
        for slot in range(out["k32t"].shape[0]):
            out["k32t"][slot, 0] = k_t
    if "k32" in emit:
        put("k32", k, F32)
    if "v32" in emit:
        put("v32", v, F32)
    if "logf" in emit:
        z = jnp.dot(hb, wf_ref[...], preferred_element_type=F32) + bf_ref[...]
        put("logf", jax.nn.log_sigmoid(z), F32)


def _qkv_proj(x, mod, g, w_qkv, wf=None, bf=None, *, bb, ts, emit, row_start=0, rows=None,
              state_slot=(0, 1), state=None):
    b, s, d = x.shape
    rows = s if rows is None else rows
    assert rows % ts == 0 and row_start % ts == 0 and b % bb == 0
    assert ts % 16 == 0 or bb == 1
    assert bb == 1 or not {"qt", "vt", "k32t", "v32t"} & set(emit)
    off = row_start // ts
    slot, n_slots = state_slot
    assert ("logf" in emit) == (wf is not None)
    row_spec = lambda width: pl.BlockSpec((bb, ts, width), lambda i, j: (i, j, 0))
    in_specs = [pl.BlockSpec((bb, ts, d), lambda i, j: (i, j + off, 0)),
                pl.BlockSpec((bb, 6, d), lambda i, j: (i, 0, 0)),
                _resident((1, d)),
                _resident((d, 3 * d))]
    args = [x, mod, g.reshape(1, d), w_qkv]
    if wf is not None:
        in_specs += [_resident((d, LANES)), _resident((1, LANES))]
        args += [wf, bf]
    kinds = {
        "q": (row_spec(d), (b, rows, d), BF16), "k": (row_spec(d), (b, rows, d), BF16),
        "v": (row_spec(d), (b, rows, d), BF16),
        "vt": (pl.BlockSpec((1, d, ts), lambda i, j: (i, 0, j)), (b, d, rows), BF16),
        "qt": (pl.BlockSpec((1, d, ts), lambda i, j: (i, 0, j)), (b, d, rows), BF16),
        "k32": (row_spec(d), (b, rows, d), F32), "v32": (row_spec(d), (b, rows, d), F32),
        "logf": (row_spec(LANES), (b, rows, LANES), F32),
    }
    if state is None:
        state_spec = pl.BlockSpec((n_slots, 1, d, ts), lambda i, j: (0, i, 0, j))
    else:
        state_spec = pl.BlockSpec((1, 1, d, ts), lambda i, j: (slot, i, 0, j))
    kinds["k32t"] = kinds["v32t"] = (state_spec, (n_slots, b, d, rows), F32)
    aliases = {}
    if state is not None:
        for name in ("k32t", "v32t"):
            aliases[len(args)] = emit.index(name)
            in_specs.append(pl.BlockSpec(memory_space=pl.ANY))
            args.append(state[name])
    outs = pl.pallas_call(
        functools.partial(_qkv_body, d=d, emit=emit, n_aliased=len(aliases)),
        grid=(b // bb, rows // ts),
        in_specs=in_specs,
        out_specs=[kinds[e][0] for e in emit],
        out_shape=[jax.ShapeDtypeStruct(kinds[e][1], kinds[e][2]) for e in emit],
        input_output_aliases=aliases,
        compiler_params=_params("parallel", "parallel"),
        name="norm_qkv",
    )(*args)
    return dict(zip(emit, outs))


def _post_attn_body(*refs, d_ff, final_norm):
    o_ref, x_ref, mod_ref, wo_ref, g_ref, wgu_ref, wd_ref = refs[:7]
    out_ref = refs[-1]
    x = x_ref[...]
    bb, ts, d = x.shape
    mod = lambda r: mod_ref[:, r:r + 1, :]
    y = jnp.dot(o_ref[...].reshape(bb * ts, d), wo_ref[...], preferred_element_type=F32)
    x1 = x + mod(2) * y.reshape(bb, ts, d)
    h = _modulated_norm(x1, g_ref[...], mod(4), mod(3))
    hb = h.reshape(bb * ts, d).astype(BF16)
    acc = jnp.zeros((bb * ts, d), F32)
    for c in range(d_ff // FF_CHUNK):
        lo = c * FF_CHUNK
        gate = jnp.dot(hb, wgu_ref[:, lo:lo + FF_CHUNK], preferred_element_type=F32)
        up = jnp.dot(hb, wgu_ref[:, d_ff + lo:d_ff + lo + FF_CHUNK], preferred_element_type=F32)
        act = (jax.nn.silu(gate) * up).astype(BF16)
        acc = acc + jnp.dot(act, wd_ref[lo:lo + FF_CHUNK, :], preferred_element_type=F32)
    x2 = x1 + mod(5) * acc.reshape(bb, ts, d)
    if final_norm:
        gout_ref = refs[7]
        ms = jnp.mean(x2 * x2, axis=-1, keepdims=True)
        x2 = x2 * lax.rsqrt(ms + RMS_EPS) * gout_ref[...]
    out_ref[...] = x2


def _post_attn(o, x, mod, w_o, g, w_gu, w_down, g_out=None, *, bb, ts):
    b, s, d = x.shape
    d_ff = w_down.shape[0]
    assert s % ts == 0 and b % bb == 0 and d_ff % FF_CHUNK == 0
    assert ts % 16 == 0 or bb == 1
    row = pl.BlockSpec((bb, ts, d), lambda i, j: (i, j, 0))
    in_specs = [row, row, pl.BlockSpec((bb, 6, d), lambda i, j: (i, 0, 0)), _resident((d, d)),
                _resident((1, d)), _resident((d, 2 * d_ff)), _resident((d_ff, d))]
    args = [o, x, mod, w_o, g.reshape(1, d), w_gu, w_down]
    if g_out is not None:
        in_specs.append(_resident((1, d)))
        args.append(g_out.reshape(1, d))
    return pl.pallas_call(
        functools.partial(_post_attn_body, d_ff=d_ff, final_norm=g_out is not None),
        grid=(b // bb, s // ts),
        in_specs=in_specs, out_specs=row,
        out_shape=jax.ShapeDtypeStruct((b, s, d), F32),
        compiler_params=_params("parallel", "parallel"),
        name="oproj_ffn",
    )(*args)


def _bias_row_body(tab_ref, g_ref):
    u = lax.broadcasted_iota(jnp.int32, (REL_PAD, G_WIDTH), 1)
    r = lax.broadcasted_iota(jnp.int32, (REL_PAD, G_WIDTH), 0)
    rel = A_PAST + (A_TILE - 1) - u
    idx = jnp.clip(rel, -(CHUNK - 1), MAX_REL) + (CHUNK - 1)
    onehot = jnp.where(r == idx, 1.0, 0.0).astype(BF16)
    hi, mid, lo = _split3(tab_ref[...])
    g_ref[...] = (jnp.dot(hi, onehot, preferred_element_type=F32)
                  + jnp.dot(mid, onehot, preferred_element_type=F32)
                  + jnp.dot(lo, onehot, preferred_element_type=F32))


def _bias_tile_body(g_ref, o_ref, ot_ref):
    x = jnp.broadcast_to(g_ref[0], (A_TILE, G_WIDTH))
    rolled = pltpu.roll(x, G_WIDTH - (A_TILE - 1), 1, stride=1, stride_axis=0)
    val = rolled[:, :A_WIN]
    qc = lax.broadcasted_iota(jnp.int32, (A_TILE, A_WIN), 0) // CHUNK
    kc = lax.broadcasted_iota(jnp.int32, (A_TILE, A_WIN), 1) // CHUNK
    band = (kc >= qc) & (kc <= qc + N_PAST_CHUNKS)
    tile = jnp.where(band, val, NEG_INF)
    o_ref[0] = tile
    ot_ref[0] = tile.T


def _rel_bias_tile(table):
    h, rel = table.shape
    assert rel == CHUNK + MAX_REL and rel <= REL_PAD and A_WIN + A_TILE - 1 <= G_WIDTH
    tab = jnp.pad(table, ((0, 0), (0, REL_PAD - rel)))
    gen = pl.pallas_call(
        _bias_row_body,
        out_shape=jax.ShapeDtypeStruct((h, G_WIDTH), F32),
        name="rel_bias_row",
    )(tab)
    return pl.pallas_call(
        _bias_tile_body,
        grid=(h,),
        in_specs=[pl.BlockSpec((1, 1, G_WIDTH), lambda i: (i, 0, 0))],
        out_specs=[pl.BlockSpec((1, A_TILE, A_WIN), lambda i: (i, 0, 0)),
                   pl.BlockSpec((1, A_WIN, A_TILE), lambda i: (i, 0, 0))],
        out_shape=[jax.ShapeDtypeStruct((h, A_TILE, A_WIN), F32),
                   jax.ShapeDtypeStruct((h, A_WIN, A_TILE), F32)],
        compiler_params=_params("parallel"),
        name="rel_bias_tile",
    )(gen.reshape(h, 1, G_WIDTH))


def _head_select(shape, hh):
    lower = _lower_lanes(shape)
    return lower if hh == 0 else jnp.logical_not(lower)


def _chunk_attn_body(q_ref, k_ref, vt_ref, bias_ref, o_ref, s0_ref, s1_ref, m0_ref, m1_ref,
                     *, n_tiles):
    sets = ((s0_ref, m0_ref), (s1_ref, m1_ref))

    def head_queries(qt):
        zero = jnp.zeros((HEAD_DIM, qt.shape[1]), qt.dtype)
        return (jnp.concatenate([qt[:HEAD_DIM], zero], axis=0),
                jnp.concatenate([zero, qt[HEAD_DIM:]], axis=0))

    def head_values(vt):
        ones = jnp.ones((HEAD_DIM, vt.shape[1]), BF16)
        return (jnp.concatenate([vt[:HEAD_DIM], ones], axis=0),
                jnp.concatenate([ones, vt[HEAD_DIM:]], axis=0))

    def finish(accs):
        o0 = accs[0][:HEAD_DIM] / accs[0][HEAD_DIM:]
        o1 = accs[1][HEAD_DIM:] / accs[1][:HEAD_DIM]
        return jnp.concatenate([o0, o1], axis=0).T.astype(o_ref.dtype)

    def edge_tile(t):
        nk = (t + 1) * A_TILE
        qm = head_queries(q_ref[0, :, t * A_TILE:(t + 1) * A_TILE])
        kw = k_ref[0, 0:nk, :]
        vh = head_values(vt_ref[0, :, 0:nk])
        accs = []
        for hh in range(HEADS_PER_BLOCK):
            s = (jnp.dot(kw, qm[hh], preferred_element_type=F32)
                 + bias_ref[hh, A_WIN - nk:A_WIN, :])
            pt = jnp.exp(s - jnp.max(s, axis=0, keepdims=True)).astype(BF16)
            accs.append(jnp.dot(vh[hh], pt, preferred_element_type=F32))
        o_ref[0, t * A_TILE:(t + 1) * A_TILE, :] = finish(accs)

    def score_step(t, dst):
        s_ref, m_ref = sets[dst]
        qm = head_queries(q_ref[0, :, pl.ds(pl.multiple_of(t * A_TILE, A_TILE), A_TILE)])
        kw = k_ref[0, pl.ds(pl.multiple_of((t - 2) * A_TILE, A_TILE), A_WIN), :]
        for hh in range(HEADS_PER_BLOCK):
            s = jnp.dot(kw, qm[hh], preferred_element_type=F32) + bias_ref[hh]
            s_ref[hh] = s
            m_ref[hh] = jnp.max(s, axis=0, keepdims=True)

    def value_step(t, src):
        s_ref, m_ref = sets[src]
        vh = head_values(vt_ref[0, :, pl.ds(pl.multiple_of((t - 2) * A_TILE, A_TILE), A_WIN)])
        accs = []
        for hh in range(HEADS_PER_BLOCK):
            pt = jnp.exp(s_ref[hh] - m_ref[hh]).astype(BF16)
            accs.append(jnp.dot(vh[hh], pt, preferred_element_type=F32))
        o_ref[0, pl.ds(pl.multiple_of(t * A_TILE, A_TILE), A_TILE), :] = finish(accs)

    edge_tile(0)
    edge_tile(1)
    score_step(2, 0)

    def two_tiles(u):
        t = 2 + 2 * u
        score_step(t + 1, 1)
        value_step(t, 0)
        score_step(t + 2, 0)
        value_step(t + 1, 1)

    def four_tiles(w, carry):
        two_tiles(2 * w)
        two_tiles(2 * w + 1)
        return carry

    pairs_of_tiles = (n_tiles - 4) // 2
    lax.fori_loop(0, pairs_of_tiles // 2, four_tiles, 0)
    if pairs_of_tiles % 2:
        two_tiles(pairs_of_tiles - 1)
    score_step(n_tiles - 1, 1)
    value_step(n_tiles - 2, 0)
    value_step(n_tiles - 1, 1)


def _chunk_attn_prompt(qt, k, vt, bias_t):
    b, s, d = k.shape
    n_tiles = s // A_TILE
    assert A_WIN == 3 * A_TILE and s % A_TILE == 0 and d % LANES == 0
    assert n_tiles >= 4 and n_tiles % 2 == 0
    pairs = d // LANES
    rows = pl.BlockSpec((1, s, LANES), lambda p, i: (i, 0, p))
    cols = pl.BlockSpec((1, LANES, s), lambda p, i: (i, p, 0))
    return pl.pallas_call(
        functools.partial(_chunk_attn_body, n_tiles=n_tiles),
        grid=(pairs, b),
        in_specs=[cols, rows, cols,
                  pl.BlockSpec((HEADS_PER_BLOCK, A_WIN, A_TILE), lambda p, i: (p, 0, 0))],
        out_specs=rows,
        out_shape=jax.ShapeDtypeStruct((b, s, d), BF16),
        scratch_shapes=[pltpu.VMEM((HEADS_PER_BLOCK, A_WIN, A_TILE), F32)] * 2
        + [pltpu.VMEM((HEADS_PER_BLOCK, 1, A_TILE), F32)] * 2,
        compiler_params=_params("parallel", "parallel"),
        name="chunk_attn",
    )(qt, k, vt, bias_t)


def _sample_attn_body(*refs, n_cache, forget):
    q_ref, kc_ref, vc_ref, kn_ref, vn_ref = refs[:5]
    o_ref = refs[-1]
    t, d = q_ref.shape[1:]
    if forget:
        fq_ref, fk_ref = refs[5:7]
        causal = (lax.broadcasted_iota(jnp.int32, (t, t), 1)
                  <= lax.broadcasted_iota(jnp.int32, (t, t), 0))
    else:
        bias_ref = refs[5]
    for p in range(d // LANES):
        lanes = slice(p * LANES, (p + 1) * LANES)
        q = q_ref[0, :, lanes]
        kc_t = kc_ref[0, 0, lanes, :].astype(BF16)
        vc_t = vc_ref[0, 0, lanes, :].astype(BF16)
        kn = kn_ref[0, :, lanes]
        vn = vn_ref[0, :, lanes]
        outs = []
        for hh in range(HEADS_PER_BLOCK):
            head = HEADS_PER_BLOCK * p + hh
            qm = jnp.where(_head_select(q.shape, hh), q, jnp.zeros_like(q))
            if forget:
                fq = fq_ref[0, :, head:head + 1]
                sc = (jnp.dot(qm, kc_t, preferred_element_type=F32)
                      + (fq - fk_ref[0, head:head + 1, 0:n_cache]))
                sn = _nt_dot(qm, kn) + (fq - fk_ref[0, head:head + 1, n_cache:n_cache + t])
                sn = jnp.where(causal, sn, NEG_INF)
            else:
                sc = (jnp.dot(qm, kc_t, preferred_element_type=F32)
                      + bias_ref[head, :, 0:n_cache])
                sn = _nt_dot(qm, kn) + bias_ref[head, :, n_cache:n_cache + t]
            m = jnp.maximum(jnp.max(sc, axis=-1, keepdims=True),
                            jnp.max(sn, axis=-1, keepdims=True))
            pc = jnp.exp(sc - m)
            pn = jnp.exp(sn - m)
            l = jnp.sum(pc, axis=-1, keepdims=True) + jnp.sum(pn, axis=-1, keepdims=True)
            o = (_nt_dot(pc.astype(BF16), vc_t)
                 + jnp.dot(pn.astype(BF16), vn, preferred_element_type=F32))
            outs.append(o / l)
        o_ref[0, :, lanes] = jnp.where(_lower_lanes(outs[0].shape), outs[0],
                                       outs[1]).astype(o_ref.dtype)


def _sample_attn(q, k, v, k_cache, v_cache, layer, *, bias=None, f_col_new=None, f_row=None):
    b, t, d = q.shape
    n_cache = k_cache.shape[3]
    new = pl.BlockSpec((1, t, d), lambda i: (i, 0, 0))
    cache = pl.BlockSpec((1, 1, d, n_cache), lambda i: (layer, i, 0, 0))
    forget = bias is None
    if forget:
        extra_specs = [pl.BlockSpec((1, t, LANES), lambda i: (i, 0, 0)),
                       pl.BlockSpec((1,) + f_row.shape[1:], lambda i: (i, 0, 0))]
        extra = [f_col_new, f_row]
    else:
        assert n_cache == A_PAST and t <= CHUNK and n_cache + t <= A_WIN
        extra_specs = [pl.BlockSpec((bias.shape[0], t, A_WIN), lambda i: (0, 0, 0))]
        extra = [bias]
    return pl.pallas_call(
        functools.partial(_sample_attn_body, n_cache=n_cache, forget=forget),
        grid=(b,),
        in_specs=[new, cache, cache, new, new] + extra_specs,
        out_specs=new,
        out_shape=jax.ShapeDtypeStruct((b, t, d), BF16),
        compiler_params=_params("parallel"),
        name="sample_attn",
    )(q, k_cache, v_cache, k, v, *extra)


def _bias_lane_selector(first, d):
    col = lax.broadcasted_iota(jnp.int32, (3 * LANES, d), 1)
    row = lax.broadcasted_iota(jnp.int32, (3 * LANES, d), 0)
    lane = col % LANES
    head = HEADS_PER_BLOCK * (col // LANES) + jnp.where(lane >= HEAD_DIM, 0, 1)
    hit = ((row % LANES) == head) & ((lane % HEAD_DIM) == first + row // LANES)
    return jnp.where(hit, 1.0, 0.0).astype(BF16)


def _bias_lanes(f, sel_q, sel_k, d):
    pieces = jnp.concatenate(_split3(f), axis=1)
    slot_row = lax.broadcasted_iota(jnp.int32, (1, d), 1) % HEAD_DIM
    ones = lambda first: jnp.where((slot_row >= first) & (slot_row < first + 3), 1.0, 0.0)
    q_side = jnp.dot(pieces, sel_q, preferred_element_type=F32) + ones(3)
    k_side = ones(0) - jnp.dot(pieces, sel_k, preferred_element_type=F32)
    return q_side, k_side


def _cumsum_body(x_ref, *refs, tc, n_heads, d, emit_bias_lanes):
    out_a_ref, out_b_ref, carry_ref, tri_ref = refs[:4]

    @pl.when(pl.program_id(1) == 0)
    def _():
        carry_ref[...] = jnp.zeros_like(carry_ref)
        r = lax.broadcasted_iota(jnp.int32, (tc, tc), 0)
        c = lax.broadcasted_iota(jnp.int32, (tc, tc), 1)
        tri_ref[...] = jnp.where(c <= r, 1.0, 0.0).astype(BF16)
        if emit_bias_lanes:
            refs[4][...] = _bias_lane_selector(0, d)
            refs[5][...] = _bias_lane_selector(3, d)

    pieces = jnp.concatenate(_split3(x_ref[0]), axis=1)
    sums = jnp.dot(tri_ref[...], pieces, preferred_element_type=F32)
    f = ((sums[:, 0:LANES] + sums[:, LANES:2 * LANES]) + sums[:, 2 * LANES:3 * LANES]
         + carry_ref[0:1, :])
    carry_ref[0:1, :] = f[tc - 1:tc, :]
    if emit_bias_lanes:
        q_side, k_side = _bias_lanes(f, refs[4][...], refs[5][...], d)
        out_a_ref[0] = q_side.T.astype(BF16)
        out_b_ref[0] = k_side.astype(BF16)
    else:
        out_a_ref[0] = f
        out_b_ref[0] = f.T[0:n_heads, :]


def _cumsum(logf, *, tc, d, emit_bias_lanes, n_heads=N_HEADS):
    b, s, w = logf.shape
    assert s % tc == 0 and tc % LANES == 0 and w == LANES
    scratch = [pltpu.VMEM((8, LANES), F32), pltpu.VMEM((tc, tc), BF16)]
    if emit_bias_lanes:
        out_specs = [pl.BlockSpec((1, d, tc), lambda i, j: (i, 0, j)),
                     pl.BlockSpec((1, tc, d), lambda i, j: (i, j, 0))]
        out_shape = [jax.ShapeDtypeStruct((b, d, s), BF16), jax.ShapeDtypeStruct((b, s, d), BF16)]
        scratch += [pltpu.VMEM((3 * LANES, d), BF16)] * 2
    else:
        out_specs = [pl.BlockSpec((1, tc, LANES), lambda i, j: (i, j, 0)),
                     pl.BlockSpec((1, n_heads, tc), lambda i, j: (i, 0, j))]
        out_shape = [jax.ShapeDtypeStruct((b, s, LANES), F32),
                     jax.ShapeDtypeStruct((b, n_heads, s), F32)]
    return pl.pallas_call(
        functools.partial(_cumsum_body, tc=tc, n_heads=n_heads, d=d,
                          emit_bias_lanes=emit_bias_lanes),
        grid=(b, s // tc),
        in_specs=[pl.BlockSpec((1, tc, LANES), lambda i, j: (i, j, 0))],
        out_specs=out_specs, out_shape=out_shape,
        scratch_shapes=scratch,
        compiler_params=_params("parallel", "arbitrary"),
        name="logf_cumsum",
    )(logf)


def _fox_body(q_ref, qa_ref, k_ref, ka_ref, vt_ref, o_ref, qh_ref, acc_ref,
              s_ref, m_ref, a_ref, *, tile, n_blocks):
    qw = tile // FOX_QUERY_SPLIT
    chains = [(hh, qs) for hh in range(HEADS_PER_BLOCK) for qs in range(FOX_QUERY_SPLIT)]
    lower = _lower_lanes((tile, LANES))
    ones = jnp.ones((HEAD_DIM, tile), BF16)
    block_rows = lambda i: pl.ds(pl.multiple_of(i * tile, tile), tile)

    def load_queries(i):
        qt = q_ref[0, :, block_rows(i)]
        qat = qa_ref[0, :, block_rows(i)]
        qh_ref[0] = jnp.concatenate([qt[:HEAD_DIM], qat[HEAD_DIM:]], axis=0)
        qh_ref[1] = jnp.concatenate([qat[:HEAD_DIM], qt[HEAD_DIM:]], axis=0)

    def score_step(j, dst, src, diagonal=False):
        off = pl.multiple_of(j * tile, tile)
        kb = k_ref[0, pl.ds(off, tile), :]
        ka = ka_ref[0, pl.ds(off, tile), :]
        kh = (jnp.where(lower, kb, ka), jnp.where(lower, ka, kb))
        for c, (hh, qs) in enumerate(chains):
            s = jnp.dot(kh[hh], qh_ref[hh, :, qs * qw:(qs + 1) * qw],
                        preferred_element_type=F32)
            if diagonal:
                key = lax.broadcasted_iota(jnp.int32, s.shape, 0)
                qry = lax.broadcasted_iota(jnp.int32, s.shape, 1) + qs * qw
                s = jnp.where(key <= qry, s, NEG_INF)
            m_prev = jnp.full((1, qw), NEG_INF, F32) if src is None else m_ref[src, c]
            m_new = jnp.maximum(m_prev, jnp.max(s, axis=0, keepdims=True))
            s_ref[dst, c] = s
            m_ref[dst, c] = m_new
            a_ref[dst, c] = jnp.exp(m_prev - m_new)

    def value_step(j, src, acc):
        off = pl.multiple_of(j * tile, tile)
        vt = vt_ref[0, :, pl.ds(off, tile)]
        vh = (jnp.concatenate([vt[:HEAD_DIM], ones], axis=0),
              jnp.concatenate([ones, vt[HEAD_DIM:]], axis=0))
        for c, (hh, qs) in enumerate(chains):
            pt = jnp.exp(s_ref[src, c] - m_ref[src, c]).astype(BF16)
            cols = slice(qs * qw, (qs + 1) * qw)
            acc_ref[acc, hh, :, cols] = (a_ref[src, c] * acc_ref[acc, hh, :, cols]
                                         + jnp.dot(vh[hh], pt, preferred_element_type=F32))

    def finish(i, acc):
        o0 = acc_ref[acc, 0, :HEAD_DIM, :] / acc_ref[acc, 0, HEAD_DIM:, :]
        o1 = acc_ref[acc, 1, HEAD_DIM:, :] / acc_ref[acc, 1, :HEAD_DIM, :]
        o_ref[0, block_rows(i), :] = jnp.concatenate([o0, o1], axis=0).T.astype(o_ref.dtype)

    def steady(i, acc, first, second, count):
        block_of_step = lambda t: jnp.where(t == 0, i, t - 1)

        def two_steps(pp):
            t = 2 * pp
            score_step(t, second, first)
            value_step(block_of_step(t), first, acc)
            score_step(t + 1, first, second)
            value_step(t, second, acc)

        def four_steps(u, carry):
            two_steps(2 * u)
            two_steps(2 * u + 1)
            return carry

        lax.fori_loop(0, count // 2, four_steps, 0)

        @pl.when(count % 2 == 1)
        def _():
            two_steps(count - 1)

    def pair_of_blocks(r, carry):
        even, odd = 2 * r, 2 * r + 1
        load_queries(even)
        acc_ref[0] = jnp.zeros(acc_ref.shape[1:], F32)
        score_step(even, 0, None, diagonal=True)
        value_step(jnp.maximum(even - 2, 0), 2, 1)
        finish(jnp.maximum(even - 1, 0), 1)
        steady(even, 0, 0, 1, r)
        load_queries(odd)
        acc_ref[1] = jnp.zeros(acc_ref.shape[1:], F32)
        score_step(odd, 1, None, diagonal=True)
        value_step(jnp.where(even == 0, even, even - 1), 0, 0)
        finish(even, 0)
        steady(odd, 1, 1, 2, r)
        score_step(odd - 1, 2, 1)
        value_step(jnp.where(odd == 1, odd, odd - 2), 1, 1)
        return carry

    s_ref[2] = jnp.zeros(s_ref.shape[1:], F32)
    m_ref[2] = jnp.zeros(m_ref.shape[1:], F32)
    a_ref[2] = jnp.zeros(a_ref.shape[1:], F32)
    acc_ref[1] = jnp.zeros(acc_ref.shape[1:], F32)
    lax.fori_loop(0, n_blocks // 2, pair_of_blocks, 0)
    value_step(n_blocks - 2, 2, 1)
    finish(n_blocks - 1, 1)


def _fox_attn_prompt(qt, qat, k, ka, vt):
    b, s, d = k.shape
    tile = FOX_TILE
    assert s % (2 * tile) == 0
    pairs = d // LANES
    n_chains = HEADS_PER_BLOCK * FOX_QUERY_SPLIT
    qw = tile // FOX_QUERY_SPLIT
    seq = pl.BlockSpec((1, s, LANES), lambda i, p: (i, 0, p))
    seq_t = pl.BlockSpec((1, LANES, s), lambda i, p: (i, p, 0))
    return pl.pallas_call(
        functools.partial(_fox_body, tile=tile, n_blocks=s // tile),
        grid=(b, pairs),
        in_specs=[seq_t, seq_t, seq, seq, seq_t],
        out_specs=seq,
        out_shape=jax.ShapeDtypeStruct((b, s, d), BF16),
        scratch_shapes=[pltpu.VMEM((HEADS_PER_BLOCK, LANES, tile), BF16),
                        pltpu.VMEM((2, HEADS_PER_BLOCK, LANES, tile), F32),
                        pltpu.VMEM((3, n_chains, tile, qw), F32),
                        pltpu.VMEM((3, n_chains, 1, qw), F32),
                        pltpu.VMEM((3, n_chains, 1, qw), F32)],
        compiler_params=_params("parallel", "parallel"),
        name="fox_attn",
    )(qt, qat, k, ka, vt)


def _trunk(x, mods, weights, a_mix, b_mix, *, bb, ts, keep, v_name):
    b, s, d = x.shape
    depth = mods.shape[0]
    n_a, n_b = (depth + 1) // 2, depth // 2
    transposed = v_name == "vt"
    k_name, v32_name = ("k32t", "v32t") if transposed else ("k32", "v32")
    q_name = "qt" if transposed else "q"
    a_state, b_state, b_logf = None, None, []
    stacked = lambda st: {k_name: jnp.stack([e[k_name] for e in st]),
                          v32_name: jnp.stack([e[v32_name] for e in st])}
    a_list, b_list = [], []
    for l in range(depth):
        mod = mods[l]
        g, w = weights["g_mix"][l], weights["w_qkv"][l]
        if l % 2 == 0:
            kw = dict(state_slot=(l // 2, n_a), state=a_state) if transposed else {}
            if keep == s:
                proj = kv32 = _qkv_proj(x, mod, g, w, bb=bb, ts=ts,
                                        emit=(q_name, "k", v_name, k_name, v32_name), **kw)
            else:
                proj = _qkv_proj(x, mod, g, w, bb=bb, ts=ts, emit=(q_name, "k", v_name))
                kv32 = _qkv_proj(x, mod, g, w, bb=bb, ts=min(ts, keep), emit=(k_name, v32_name),
                                 row_start=s - keep, rows=keep, **kw)
            a_state = kv32
            a_list.append(kv32)
            o = a_mix(l // 2, proj)
        else:
            ib = l // 2
            kw = dict(state_slot=(ib, n_b), state=b_state) if transposed else {}
            proj = _qkv_proj(x, mod, g, w, weights["w_fgate"][ib], weights["b_fgate"][ib],
                             bb=bb, ts=ts, emit=(q_name, "k", v_name, k_name, v32_name, "logf"),
                             **kw)
            b_state = proj
            b_list.append(proj)
            b_logf.append(proj["logf"][:, :, :N_HEADS])
            o = b_mix(ib, proj)
        g_out = weights["g_out"] if l == depth - 1 else None
        x = _post_attn(o, x, mod, weights["w_o"][l], weights["g_ffn"][l], weights["w_gu"][l],
                       weights["w_down"][l], g_out, bb=bb, ts=ts)
    if not transposed:
        a_state, b_state = stacked(a_list), stacked(b_list)

    def heads(a):
        if transposed:
            n, bsz, _, r = a.shape
            return a.reshape(n, bsz, N_HEADS, HEAD_DIM, r).transpose(0, 1, 4, 2, 3)
        return a.reshape(a.shape[:3] + (N_HEADS, HEAD_DIM))

    return (x, heads(a_state[k_name]), heads(a_state[v32_name]),
            heads(b_state[k_name]), heads(b_state[v32_name]), jnp.stack(b_logf))


def kernel(x_prompt, x_sample, cache_a_k, cache_a_v, cache_b_k, cache_b_v, cache_b_logf,
           c_prompt, c_sample, w_mod, b_mod, g_mix, g_ffn, w_qkv, w_o, rel_bias,
           w_fgate, b_fgate, w_gu, w_down, g_out):
    bp, sp, d = x_prompt.shape
    bs, t_new, _ = x_sample.shape
    depth = w_mod.shape[0]
    n_b = w_fgate.shape[0]
    assert d == N_HEADS * HEAD_DIM
    keep = min(A_PAST, sp)

    weights = {
        "g_mix": g_mix, "g_ffn": g_ffn, "g_out": g_out,
        "w_qkv": w_qkv.astype(BF16), "w_o": w_o.astype(BF16),
        "w_gu": w_gu.astype(BF16), "w_down": w_down.astype(BF16),
        "w_fgate": jnp.pad(w_fgate, ((0, 0), (0, 0), (0, LANES - N_HEADS))).astype(BF16),
        "b_fgate": jnp.pad(b_fgate, ((0, 0), (0, LANES - N_HEADS))).reshape(n_b, 1, LANES),
    }

    c_all = jnp.concatenate([c_prompt, c_sample], axis=0)
    rows = -(-c_all.shape[0] // 16) * 16
    c_all = jnp.pad(c_all, ((0, rows - c_all.shape[0]), (0, 0)))
    mods = _modulation(c_all, w_mod, b_mod)
    mods_p = mods[:, :bp].reshape(depth, bp, 6, d)
    mods_s = mods[:, bp:bp + bs].reshape(depth, bs, 6, d)

    bias = [_rel_bias_tile(rel_bias[i]) for i in range(rel_bias.shape[0])]

    def a_prompt(i, proj):
        return _chunk_attn_prompt(proj["qt"], proj["k"], proj["vt"], bias[i][1])

    def b_prompt(i, proj):
        qa, ka = _cumsum(proj["logf"], tc=FOX_TILE, d=d, emit_bias_lanes=True)
        return _fox_attn_prompt(proj["qt"], qa, proj["k"], ka, proj["vt"])

    prompt_out = _trunk(x_prompt, mods_p, weights, a_prompt, b_prompt, bb=1, ts=ROW_TILE,
                              keep=keep, v_name="vt")

    by_feature = lambda c: c.transpose(0, 1, 3, 4, 2).reshape(c.shape[0], bs, d, c.shape[2])
    ca_k, ca_v = by_feature(cache_a_k), by_feature(cache_a_v)
    cb_k, cb_v = by_feature(cache_b_k), by_feature(cache_b_v)
    past = cache_b_k.shape[2]
    total = -(-(past + t_new) // SAMPLE_CUMSUM_TILE) * SAMPLE_CUMSUM_TILE

    def a_sample(i, proj):
        return _sample_attn(proj["q"], proj["k"], proj["v"], ca_k, ca_v, i, bias=bias[i][0])

    def b_sample(i, proj):
        lf_cache = jnp.pad(cache_b_logf[i].astype(F32), ((0, 0), (0, 0), (0, LANES - N_HEADS)))
        lf = jnp.concatenate([lf_cache, proj["logf"]], axis=1)
        lf = jnp.pad(lf, ((0, 0), (0, total - (past + t_new)), (0, 0)))
        f_col, f_row = _cumsum(lf, tc=SAMPLE_CUMSUM_TILE, d=d, emit_bias_lanes=False)
        return _sample_attn(proj["q"], proj["k"], proj["v"], cb_k, cb_v, i,
                            f_col_new=f_col[:, past:past + t_new], f_row=f_row)

    sample_out = _trunk(x_sample, mods_s, weights, a_sample, b_sample, bb=bs, ts=t_new,
                              keep=t_new, v_name="v")

    return (prompt_out[0], sample_out[0]) + tuple(prompt_out[1:]) + tuple(sample_out[1:])
```
